```python
import math
import jax, jax.numpy as jnp
from jax import lax
import numpy as np

D_MODEL = 1024
BATCH = 2
SEQ = 8192
DEPTH = 4

SSM_WIDTH = 512
SSM_GROUP = 16
SSM_GROUPS = SSM_WIDTH // SSM_GROUP
SSM_STATE = 64
MLSTM_HEADS = 4
MLSTM_HEAD_DIM = 128
MLSTM_WIDTH = MLSTM_HEADS * MLSTM_HEAD_DIM
MLSTM_CHUNK = 64
CONV_WIDTH = 4
ATTN_HEADS = 4
ATTN_HEAD_DIM = 128
ATTN_WIDTH = ATTN_HEADS * ATTN_HEAD_DIM
Q_LORA_RANK = 256
IDX_HEADS = 8
IDX_HEAD_DIM = 64
TOPK_MAX = 256
Q_BLOCK = 128
N_BRANCH = 3
BRANCH_WIDTH = 512
EPS = 1e-6

SPLITS = (SSM_WIDTH, SSM_WIDTH,
          MLSTM_WIDTH, MLSTM_WIDTH, MLSTM_WIDTH, MLSTM_HEADS, MLSTM_HEADS, MLSTM_WIDTH,
          Q_LORA_RANK, ATTN_HEAD_DIM, ATTN_HEAD_DIM, IDX_HEAD_DIM, IDX_HEADS, ATTN_WIDTH,
          N_BRANCH * D_MODEL)
IN_WIDTH = sum(SPLITS)

kernel_name = 'hybrid_s5_mlstm_dsa_gated_block'


def rmsnorm(x, g):
    xf = x.astype(jnp.float32)
    y = xf * lax.rsqrt(jnp.mean(xf * xf, axis=-1, keepdims=True) + EPS)
    return (y * g.astype(jnp.float32)).astype(x.dtype)


def layernorm(x, g, b):
    xf = x.astype(jnp.float32)
    mu = jnp.mean(xf, axis=-1, keepdims=True)
    xc = xf - mu
    y = xc * lax.rsqrt(jnp.mean(xc * xc, axis=-1, keepdims=True) + EPS)
    return (y * g.astype(jnp.float32) + b.astype(jnp.float32)).astype(x.dtype)


def causal_depthwise_conv(x, w, b):
    k_w, ch = w.shape
    y = lax.conv_general_dilated(x, w[:, None, :], window_strides=(1,), padding=[(k_w - 1, 0)],
                                 dimension_numbers=('NWC', 'WIO', 'NWC'), feature_group_count=ch)
    return y + b


def s5_mixer(u, a_re, a_im, b_re, b_im, c_re, c_im, log_dt, d_skip, glu_w, glu_b):
    bsz, seq, _ = u.shape
    f32 = jnp.float32
    uf = u.astype(f32).reshape(bsz, seq, SSM_GROUPS, SSM_GROUP)
    dt = jnp.exp(log_dt.astype(f32))[:, None]
    ar = a_re.astype(f32)
    ai = a_im.astype(f32)
    mag = jnp.exp(ar * dt)
    ang = ai * dt
    abar_re = mag * jnp.cos(ang)
    abar_im = mag * jnp.sin(ang)
    den = ar * ar + ai * ai
    fr = ((abar_re - 1.0) * ar + abar_im * ai) / den
    fi = (abar_im * ar - (abar_re - 1.0) * ai) / den
    br = b_re.astype(f32)
    bi = b_im.astype(f32)
    bb_re = fr[..., None] * br - fi[..., None] * bi
    bb_im = fr[..., None] * bi + fi[..., None] * br
    bu_re = jnp.einsum('blgp,gnp->blgn', uf, bb_re)
    bu_im = jnp.einsum('blgp,gnp->blgn', uf, bb_im)
    a_seq_re = jnp.broadcast_to(abar_re, bu_re.shape)
    a_seq_im = jnp.broadcast_to(abar_im, bu_im.shape)

    def combine(e1, e2):
        a1r, a1i, b1r, b1i = e1
        a2r, a2i, b2r, b2i = e2
        return (a2r * a1r - a2i * a1i, a2r * a1i + a2i * a1r,
                a2r * b1r - a2i * b1i + b2r, a2r * b1i + a2i * b1r + b2i)

    _, _, h_re, h_im = lax.associative_scan(combine, (a_seq_re, a_seq_im, bu_re, bu_im), axis=1)
    y = (jnp.einsum('blgn,gpn->blgp', h_re, c_re.astype(f32))
         - jnp.einsum('blgn,gpn->blgp', h_im, c_im.astype(f32)))
    y = y.reshape(bsz, seq, SSM_WIDTH) + d_skip.astype(f32) * u.astype(f32)
    g = jax.nn.gelu(y)
    out = g * jax.nn.sigmoid(g @ glu_w.astype(f32) + glu_b.astype(f32))
    return out.astype(u.dtype)


def mlstm_mixer(q, k, v, i_pre, f_pre, conv_w, conv_b, igate_b, fgate_b, mh_norm_g):
    bsz, seq, _ = q.shape
    f32 = jnp.float32
    qk = jax.nn.silu(causal_depthwise_conv(jnp.concatenate([q, k], axis=-1), conv_w, conv_b))
    q, k = jnp.split(qk, 2, axis=-1)
    nc = seq // MLSTM_CHUNK

    def to_chunks(t):
        t = t.astype(f32).reshape(bsz, nc, MLSTM_CHUNK, MLSTM_HEADS, -1)
        return t.transpose(1, 0, 3, 2, 4)

    def gates_to_chunks(t):
        return t.reshape(bsz, nc, MLSTM_CHUNK, MLSTM_HEADS).transpose(1, 0, 3, 2)

    qc = to_chunks(q)
    kc = to_chunks(k) * (MLSTM_HEAD_DIM ** -0.5)
    vc = to_chunks(v)
    ig = gates_to_chunks(i_pre.astype(f32) + igate_b.astype(f32))
    lf = gates_to_chunks(jax.nn.log_sigmoid(f_pre.astype(f32) + fgate_b.astype(f32)))
    causal = jnp.tril(jnp.ones((MLSTM_CHUNK, MLSTM_CHUNK), dtype=bool))

    def step(carry, xs):
        c_prev, n_prev, m_prev = carry
        qb, kb, vb, ib, fb = xs
        bcum = jnp.cumsum(fb, axis=-1)
        dmat = bcum[..., :, None] - bcum[..., None, :] + ib[..., None, :]
        dmat = jnp.where(causal, dmat, -jnp.inf)
        inter = bcum + m_prev[..., None]
        m = jnp.maximum(inter, jnp.max(dmat, axis=-1))
        dexp = jnp.exp(dmat - m[..., None])
        inter_w = jnp.exp(inter - m)
        s = jnp.einsum('bhtd,bhsd->bhts', qb, kb) * dexp
        num = (jnp.einsum('bhts,bhse->bhte', s, vb)
               + inter_w[..., None] * jnp.einsum('bhtd,bhde->bhte', qb, c_prev))
        den = jnp.sum(s, axis=-1) + inter_w * jnp.einsum('bhtd,bhd->bht', qb, n_prev)
        h = num / jnp.maximum(jnp.abs(den), jnp.exp(-m))[..., None]
        b_last = bcum[..., -1]
        g_end = b_last[..., None] - bcum + ib
        m_next = jnp.maximum(b_last + m_prev, jnp.max(g_end, axis=-1))
        w_s = jnp.exp(g_end - m_next[..., None])
        decay = jnp.exp(b_last + m_prev - m_next)
        c_next = decay[..., None, None] * c_prev + jnp.einsum('bhs,bhsd,bhse->bhde', w_s, kb, vb)
        n_next = decay[..., None] * n_prev + jnp.einsum('bhs,bhsd->bhd', w_s, kb)
        return (c_next, n_next, m_next), h

    init = (jnp.zeros((bsz, MLSTM_HEADS, MLSTM_HEAD_DIM, MLSTM_HEAD_DIM), f32),
            jnp.zeros((bsz, MLSTM_HEADS, MLSTM_HEAD_DIM), f32),
            jnp.zeros((bsz, MLSTM_HEADS), f32))
    _, hs = lax.scan(step, init, (qc, kc, vc, ig, lf))
    h = hs.transpose(1, 0, 3, 2, 4).reshape(bsz, seq, MLSTM_HEADS, MLSTM_HEAD_DIM)
    h = h * lax.rsqrt(jnp.mean(h * h, axis=-1, keepdims=True) + EPS)
    h = h.reshape(bsz, seq, MLSTM_WIDTH) * mh_norm_g.astype(f32)
    return h.astype(q.dtype)


def dsa_mixer(cq, k, v, kidx, widx, q_norm_g, w_uq, w_qidx, kidx_norm_g, kidx_norm_b):
    bsz, seq, _ = cq.shape
    f32 = jnp.float32
    cqn = rmsnorm(cq, q_norm_g)
    q = (cqn @ w_uq).reshape(bsz, seq, ATTN_HEADS, ATTN_HEAD_DIM)
    qi = (cqn @ w_qidx).reshape(bsz, seq, IDX_HEADS, IDX_HEAD_DIM)
    ki = layernorm(kidx, kidx_norm_g, kidx_norm_b).astype(f32)
    wi = widx.astype(f32) * (IDX_HEADS ** -0.5) * (IDX_HEAD_DIM ** -0.5)
    topk = min(TOPK_MAX, seq // 4)
    nb = seq // Q_BLOCK
    q_blk = q.reshape(bsz, nb, Q_BLOCK, ATTN_HEADS, ATTN_HEAD_DIM).transpose(1, 0, 2, 3, 4)
    qi_blk = qi.reshape(bsz, nb, Q_BLOCK, IDX_HEADS, IDX_HEAD_DIM).transpose(1, 0, 2, 3, 4)
    wi_blk = wi.reshape(bsz, nb, Q_BLOCK, IDX_HEADS).transpose(1, 0, 2, 3)
    starts = jnp.arange(nb, dtype=jnp.int32) * Q_BLOCK
    s_pos = jnp.arange(seq, dtype=jnp.int32)
    gather = jax.vmap(lambda kb_, ib_: kb_[ib_])

    def block(args):
        start, qb, qib, wb = args
        t = start + jnp.arange(Q_BLOCK, dtype=jnp.int32)
        rel = jax.nn.relu(jnp.einsum('bthd,bsd->bths', qib.astype(f32), ki))
        score = jnp.einsum('bth,bths->bts', wb, rel)
        causal = s_pos[None, :] <= t[:, None]
        score = jnp.where(causal[None], score, -jnp.inf)
        _, sel = lax.top_k(score, topk)
        valid = sel <= t[None, :, None]
        kg = gather(k, sel).astype(f32)
        vg = gather(v, sel).astype(f32)
        logits = jnp.einsum('bthd,btkd->bthk', qb.astype(f32), kg) * (ATTN_HEAD_DIM ** -0.5)
        logits = jnp.where(valid[:, :, None, :], logits, -jnp.inf)
        p = jax.nn.softmax(logits, axis=-1)
        return jnp.einsum('bthk,btkd->bthd', p, vg)

    o = lax.map(block, (starts, q_blk, qi_blk, wi_blk))
    o = o.transpose(1, 0, 2, 3, 4).reshape(bsz, seq, ATTN_WIDTH)
    return o.astype(cq.dtype)


def hybrid_layer(x, norm_g, w_in, ssm_a_re, ssm_a_im, ssm_b_re, ssm_b_im, ssm_c_re, ssm_c_im,
                 ssm_log_dt, ssm_d, glu_w, glu_b, qk_conv_w, qk_conv_b, igate_b, fgate_b,
                 mh_norm_g, q_norm_g, w_uq, w_qidx, kidx_norm_g, kidx_norm_b, w_branch, w_out):
    bsz, seq, _ = x.shape
    h = rmsnorm(x, norm_g)
    proj = h @ w_in
    idx = np.cumsum(np.array(SPLITS))[:-1].tolist()
    (u_a, z_a, q_b, k_b, v_b, i_b, f_b, z_b,
     cq_c, k_c, v_c, kidx_c, widx_c, z_c, gate_pre) = jnp.split(proj, idx, axis=-1)
    y_a = s5_mixer(u_a, ssm_a_re, ssm_a_im, ssm_b_re, ssm_b_im, ssm_c_re, ssm_c_im,
                   ssm_log_dt, ssm_d, glu_w, glu_b) * jax.nn.silu(z_a)
    y_b = mlstm_mixer(q_b, k_b, v_b, i_b, f_b, qk_conv_w, qk_conv_b, igate_b, fgate_b,
                      mh_norm_g) * jax.nn.silu(z_b)
    y_c = dsa_mixer(cq_c, k_c, v_c, kidx_c, widx_c, q_norm_g, w_uq, w_qidx,
                    kidx_norm_g, kidx_norm_b) * jax.nn.silu(z_c)
    ys = jnp.stack([y_a, y_b, y_c], axis=2)
    branch = jnp.einsum('blnw,nwd->blnd', ys, w_branch)
    gates = jax.nn.sigmoid(gate_pre.reshape(bsz, seq, N_BRANCH, D_MODEL))
    merged = jnp.sum(gates * branch, axis=2)
    return x + merged @ w_out


def setup_inputs(seed: int = 0) -> dict:
    key = jax.random.key(seed)
    ks = jax.random.split(key, 32)
    f32 = jnp.float32
    L_, D_ = DEPTH, D_MODEL
    G, N, P = SSM_GROUPS, SSM_STATE, SSM_GROUP

    def nrm(k, shape, scale):
        return jax.random.normal(k, shape, f32) * scale

    n_idx = jnp.arange(N, dtype=f32)
    inp = {}
    inp['x'] = jax.random.normal(ks[0], (BATCH, SEQ, D_), f32)
    inp['norm_g'] = 1.0 + nrm(ks[1], (L_, D_), 0.02)
    inp['w_in'] = nrm(ks[2], (L_, D_, IN_WIDTH), D_ ** -0.5)
    inp['ssm_a_re'] = -0.5 + nrm(ks[3], (L_, G, N), 0.01)
    inp['ssm_a_im'] = math.pi * n_idx + nrm(ks[4], (L_, G, N), 0.01)
    inp['ssm_b_re'] = nrm(ks[5], (L_, G, N, P), (2.0 * P) ** -0.5)
    inp['ssm_b_im'] = nrm(ks[6], (L_, G, N, P), (2.0 * P) ** -0.5)
    inp['ssm_c_re'] = nrm(ks[7], (L_, G, P, N), (2.0 * N) ** -0.5)
    inp['ssm_c_im'] = nrm(ks[8], (L_, G, P, N), (2.0 * N) ** -0.5)
    inp['ssm_log_dt'] = jax.random.uniform(ks[9], (L_, G), f32, math.log(1e-3), math.log(1e-1))
    inp['ssm_d'] = nrm(ks[10], (L_, SSM_WIDTH), 1.0)
    inp['glu_w'] = nrm(ks[11], (L_, SSM_WIDTH, SSM_WIDTH), SSM_WIDTH ** -0.5)
    inp['glu_b'] = nrm(ks[12], (L_, SSM_WIDTH), 0.02)
    inp['qk_conv_w'] = nrm(ks[13], (L_, CONV_WIDTH, 2 * MLSTM_WIDTH), CONV_WIDTH ** -0.5)
    inp['qk_conv_b'] = nrm(ks[14], (L_, 2 * MLSTM_WIDTH), 0.02)
    inp['igate_b'] = nrm(ks[15], (L_, MLSTM_HEADS), 0.1)
    inp['fgate_b'] = jnp.linspace(3.0, 6.0, MLSTM_HEADS, dtype=f32) + nrm(ks[16], (L_, MLSTM_HEADS), 0.1)
    inp['mh_norm_g'] = 1.0 + nrm(ks[17], (L_, MLSTM_WIDTH), 0.02)
    inp['q_norm_g'] = 1.0 + nrm(ks[18], (L_, Q_LORA_RANK), 0.02)
    inp['w_uq'] = nrm(ks[19], (L_, Q_LORA_RANK, ATTN_WIDTH), Q_LORA_RANK ** -0.5)
    inp['w_qidx'] = nrm(ks[20], (L_, Q_LORA_RANK, IDX_HEADS * IDX_HEAD_DIM), Q_LORA_RANK ** -0.5)
    inp['kidx_norm_g'] = 1.0 + nrm(ks[21], (L_, IDX_HEAD_DIM), 0.02)
    inp['kidx_norm_b'] = nrm(ks[22], (L_, IDX_HEAD_DIM), 0.02)
    inp['w_branch'] = nrm(ks[23], (L_, N_BRANCH, BRANCH_WIDTH, D_), BRANCH_WIDTH ** -0.5)
    inp['w_out'] = nrm(ks[24], (L_, D_, D_), D_ ** -0.5)
    inp['final_norm_g'] = 1.0 + nrm(ks[25], (D_,), 0.02)
    return inp


def reference(x, norm_g, w_in, ssm_a_re, ssm_a_im, ssm_b_re, ssm_b_im, ssm_c_re, ssm_c_im,
              ssm_log_dt, ssm_d, glu_w, glu_b, qk_conv_w, qk_conv_b, igate_b, fgate_b,
              mh_norm_g, q_norm_g, w_uq, w_qidx, kidx_norm_g, kidx_norm_b, w_branch, w_out,
              final_norm_g):
    for l in range(DEPTH):
        x = hybrid_layer(x, norm_g[l], w_in[l], ssm_a_re[l], ssm_a_im[l], ssm_b_re[l], ssm_b_im[l],
                         ssm_c_re[l], ssm_c_im[l], ssm_log_dt[l], ssm_d[l], glu_w[l], glu_b[l],
                         qk_conv_w[l], qk_conv_b[l], igate_b[l], fgate_b[l], mh_norm_g[l],
                         q_norm_g[l], w_uq[l], w_qidx[l], kidx_norm_g[l], kidx_norm_b[l],
                         w_branch[l], w_out[l])
    return rmsnorm(x, final_norm_g)
```

```python
import functools
import math

import jax
import jax.numpy as jnp
from jax import lax
from jax.experimental import pallas as pl
from jax.experimental.pallas import tpu as pltpu

F32 = jnp.float32
BF16 = jnp.bfloat16

D_MODEL = 1024
EPS = 1e-6
SSM_WIDTH = 512
SSM_GROUP = 16
SSM_GROUPS = 32
SSM_STATE = 64
SSM_LANES = SSM_GROUPS * SSM_STATE
SSM_BLOCKS = 4
MLSTM_HEADS = 4
MLSTM_HEAD_DIM = 128
MLSTM_WIDTH = 512
CONV_WIDTH = 4
ATTN_HEADS = 4
ATTN_HEAD_DIM = 128
ATTN_WIDTH = 512
Q_LORA_RANK = 256
IDX_HEADS = 8
IDX_HEAD_DIM = 64
TOPK_MAX = 256
Q_BLOCK = 128
N_BRANCH = 3

SUBLANES = 8
LANES = 128
S5_TILE = 256
MLSTM_TILE = 256
PREP_TILE = 256
MERGE_TILE = 256
KEY_TILE = 512
NEG_BIG = -1e30
VMEM_LIMIT = 56 * 1024 * 1024
MAX_BISECT = 64


def _dot(a, b):
    return jnp.dot(a, b, preferred_element_type=F32)


def _dot_nt(a, b):
    return lax.dot_general(a, b, (((1,), (1,)), ((), ())), preferred_element_type=F32)


def _dot_f32(a, b):
    return jnp.dot(a, b, preferred_element_type=F32, precision=lax.Precision.HIGHEST)


def _rmsnorm(x, g):
    return x * lax.rsqrt(jnp.mean(x * x, axis=-1, keepdims=True) + EPS) * g


def _silu(x):
    return x * jax.nn.sigmoid(x)


def _params(*sem):
    return pltpu.CompilerParams(dimension_semantics=sem, vmem_limit_bytes=VMEM_LIMIT)


def _full(shape):
    n = len(shape)
    return pl.BlockSpec(shape, lambda *_: (0,) * n)


def _s5_kernel(x_ref, ng_ref, wa_ref, wbu_ref, cst_ref, wc_ref, d_ref, gw_ref, gb_ref,
               out_ref, bre, bim, car):
    tl = x_ref.shape[0]

    @pl.when(pl.program_id(1) == 0)
    def _():
        car[...] = jnp.zeros_like(car)

    h = _rmsnorm(x_ref[...], ng_ref[...]).astype(BF16)
    pa = _dot(h, wa_ref[...])
    u = pa[:, :SSM_WIDTH]
    z = pa[:, SSM_WIDTH:]
    ub = u.astype(BF16)
    half = SSM_LANES // SSM_BLOCKS
    for k in range(SSM_BLOCKS):
        blk = _dot(ub[:, k * LANES:(k + 1) * LANES], wbu_ref[k])
        bre[:, k * half:(k + 1) * half] = blk[:, :half]
        bim[:, k * half:(k + 1) * half] = blk[:, half:]

    def body(r, c):
        cr, ci = c
        rows = pl.ds(pl.multiple_of(r * SUBLANES, SUBLANES), SUBLANES)
        xr = bre[rows, :]
        xi = bim[rows, :]
        for lvl in range(3):
            ar = cst_ref[2 * lvl]
            ai = cst_ref[2 * lvl + 1]
            sr = pltpu.roll(xr, 1 << lvl, 0)
            si = pltpu.roll(xi, 1 << lvl, 0)
            xr, xi = xr + ar * sr - ai * si, xi + ar * si + ai * sr
        pr = cst_ref[6]
        pi = cst_ref[7]
        hr = xr + pr * cr - pi * ci
        hi = xi + pr * ci + pi * cr
        bre[rows, :] = hr
        bim[rows, :] = hi
        last = SUBLANES - 1
        return (jnp.broadcast_to(hr[last:, :], hr.shape), jnp.broadcast_to(hi[last:, :], hi.shape))

    cr, ci = lax.fori_loop(0, tl // SUBLANES, body, (car[0], car[1]))
    car[0] = cr
    car[1] = ci

    hre = bre[...].astype(BF16)
    him = bim[...].astype(BF16)
    ys = []
    for k in range(SSM_BLOCKS):
        hk = jnp.concatenate([hre[:, k * half:(k + 1) * half], him[:, k * half:(k + 1) * half]], axis=1)
        ys.append(_dot(hk, wc_ref[k]))
    y = jnp.concatenate(ys, axis=1) + d_ref[...] * u
    g = jax.nn.gelu(y)
    out = g * jax.nn.sigmoid(_dot(g.astype(BF16), gw_ref[...]) + gb_ref[...])
    out_ref[...] = (out * _silu(z)).astype(out_ref.dtype)


def _s5_tables(a_re, a_im, b_re, b_im, c_re, c_im, log_dt):
    dt = jnp.exp(log_dt.astype(F32))[:, None]
    ar = a_re.astype(F32)
    ai = a_im.astype(F32)

    def apow(k):
        mag = jnp.exp(k * ar * dt)
        ang = k * ai * dt
        return (mag * jnp.cos(ang)).reshape(-1), (mag * jnp.sin(ang)).reshape(-1)

    abr, abi = (mag.reshape(ar.shape) for mag in apow(1))
    den = ar * ar + ai * ai
    fr = ((abr - 1.0) * ar + abi * ai) / den
    fi = (abi * ar - (abr - 1.0) * ai) / den
    br = b_re.astype(F32)
    bi = b_im.astype(F32)
    bb_re = fr[..., None] * br - fi[..., None] * bi
    bb_im = fr[..., None] * bi + fi[..., None] * br
    gpb = SSM_GROUPS // SSM_BLOCKS
    eye = jnp.eye(gpb, dtype=F32)

    def bu_block(bb):
        bb4 = bb.reshape(SSM_BLOCKS, gpb, SSM_STATE, SSM_GROUP)
        w = jnp.einsum('kgnp,gh->kgphn', bb4, eye)
        return w.reshape(SSM_BLOCKS, gpb * SSM_GROUP, gpb * SSM_STATE)

    wbu = jnp.concatenate([bu_block(bb_re), bu_block(bb_im)], axis=2).astype(BF16)

    def c_block(c):
        c4 = c.astype(F32).reshape(SSM_BLOCKS, gpb, SSM_GROUP, SSM_STATE)
        w = jnp.einsum('kgpn,gh->kgnhp', c4, eye)
        return w.reshape(SSM_BLOCKS, gpb * SSM_STATE, gpb * SSM_GROUP)

    wc = jnp.concatenate([c_block(c_re), -c_block(c_im)], axis=1).astype(BF16)

    row = jnp.arange(SUBLANES, dtype=F32)[:, None]
    tabs = []
    for lvl in range(3):
        k = 1 << lvl
        pr, pi = apow(float(k))
        keep = row >= k
        tabs += [jnp.where(keep, pr[None, :], 0.0), jnp.where(keep, pi[None, :], 0.0)]
    prs, pis = zip(*[apow(float(r + 1)) for r in range(SUBLANES)])
    tabs += [jnp.stack(prs), jnp.stack(pis)]
    cst = jnp.stack(tabs).astype(F32)
    return wbu, cst, wc


def _s5_call(x2, ng, wa, wbu, cst, wc, dskip, gw, gb, batch, seq):
    tl = S5_TILE
    nt = seq // tl
    return pl.pallas_call(
        _s5_kernel,
        grid=(batch, nt),
        in_specs=[
            pl.BlockSpec((tl, D_MODEL), lambda b, i: (b * nt + i, 0)),
            _full(ng.shape), _full(wa.shape), _full(wbu.shape), _full(cst.shape), _full(wc.shape),
            _full(dskip.shape), _full(gw.shape), _full(gb.shape),
        ],
        out_specs=pl.BlockSpec((tl, SSM_WIDTH), lambda b, i: (b * nt + i, 0)),
        out_shape=jax.ShapeDtypeStruct((batch * seq, SSM_WIDTH), BF16),
        scratch_shapes=[
            pltpu.VMEM((tl, SSM_LANES), F32),
            pltpu.VMEM((tl, SSM_LANES), F32),
            pltpu.VMEM((2, SUBLANES, SSM_LANES), F32),
        ],
        compiler_params=_params("arbitrary", "arbitrary"),
        name="s5",
    )(x2, ng, wa, wbu, cst, wc, dskip, gw, gb)


def _mlstm_kernel(x_ref, ng_ref, wb_ref, wg_ref, cw_ref, cb_ref, gbc_ref, gbr_ref, mhg_ref,
                  out_ref, qk_scr, st_scr, m_scr):
    ct = x_ref.shape[0]
    hd = MLSTM_HEAD_DIM

    @pl.when(pl.program_id(1) == 0)
    def _():
        qk_scr[...] = jnp.zeros_like(qk_scr)
        st_scr[...] = jnp.zeros_like(st_scr)
        m_scr[...] = jnp.zeros_like(m_scr)

    h32 = _rmsnorm(x_ref[...], ng_ref[...])
    hb = h32.astype(BF16)
    pb = _dot(hb, wb_ref[...])
    gates = _dot_f32(h32, wg_ref[...])

    qk_scr[0:SUBLANES, :] = qk_scr[ct:ct + SUBLANES, :]
    qk_scr[SUBLANES:, :] = pb[:, :2 * MLSTM_WIDTH]
    conv = cb_ref[...]
    for j in range(CONV_WIDTH):
        off = SUBLANES - (CONV_WIDTH - 1) + j
        conv = conv + cw_ref[j:j + 1, :] * qk_scr[off:off + ct, :]
    qk = _silu(conv)
    q = qk[:, :MLSTM_WIDTH].astype(BF16)
    k = (qk[:, MLSTM_WIDTH:] * (hd ** -0.5)).astype(BF16)
    v = pb[:, 2 * MLSTM_WIDTH:3 * MLSTM_WIDTH].astype(BF16)
    z = pb[:, 3 * MLSTM_WIDTH:]

    gc = gates + gbc_ref[...]
    gr = gates.T[:SUBLANES, :] + gbr_ref[...]

    def logsig(t):
        return jnp.minimum(t, 0.0) - jnp.log1p(jnp.exp(-jnp.abs(t)))

    ti = lax.broadcasted_iota(jnp.int32, (ct, ct), 0)
    si = lax.broadcasted_iota(jnp.int32, (ct, ct), 1)
    causal = si <= ti
    lower = causal.astype(F32)
    upper = (ti <= si).astype(F32)
    bcum_c = _dot_f32(lower, logsig(gc))
    bcum_r = _dot_f32(logsig(gr), upper)
    ones_v = jnp.ones((ct, hd), BF16)

    outs = []
    for hh in range(MLSTM_HEADS):
        lo = hh * hd
        a_c = bcum_c[:, MLSTM_HEADS + hh:MLSTM_HEADS + hh + 1]
        i_c = gc[:, hh:hh + 1]
        a_r = bcum_r[MLSTM_HEADS + hh:MLSTM_HEADS + hh + 1, :]
        i_r = gr[hh:hh + 1, :]
        m_prev = m_scr[hh:hh + 1, 0:1]
        b_last = a_c[ct - 1:ct, :]

        dmat = jnp.where(causal, a_c - a_r + i_r, -jnp.inf)
        inter = a_c + m_prev
        m = jnp.maximum(inter, jnp.max(dmat, axis=-1, keepdims=True))
        dexp = jnp.exp(dmat - m)
        inter_w = jnp.exp(inter - m)

        qh = q[:, lo:lo + hd]
        kh = k[:, lo:lo + hd]
        v_aug = jnp.concatenate([v[:, lo:lo + hd], ones_v], axis=1)
        s = (_dot_nt(qh, kh) * dexp).astype(BF16)
        st = st_scr[hh]
        numden = _dot(s, v_aug) + inter_w * _dot(qh, st.astype(BF16))
        num = numden[:, :hd]
        den = numden[:, hd:hd + 1]
        hout = num / jnp.maximum(jnp.abs(den), jnp.exp(-m))

        g_end = b_last - a_c + i_c
        m_next = jnp.maximum(b_last + m_prev, jnp.max(b_last - a_r + i_r, axis=-1, keepdims=True))
        w_s = jnp.exp(g_end - m_next)
        decay = jnp.exp(b_last + m_prev - m_next)
        kw = (kh.astype(F32) * w_s).T.astype(BF16)
        st_scr[hh] = decay * st + _dot(kw, v_aug)
        m_scr[hh:hh + 1, :] = jnp.broadcast_to(m_next, (1, LANES))

        hn = hout * lax.rsqrt(jnp.mean(hout * hout, axis=-1, keepdims=True) + EPS)
        outs.append(hn)
    hcat = jnp.concatenate(outs, axis=1) * mhg_ref[...]
    out_ref[...] = (hcat * _silu(z)).astype(out_ref.dtype)


def _mlstm_call(x2, ng, wb, wg, cw, cb, gbc, gbr, mhg, batch, seq):
    ct = MLSTM_TILE
    nt = seq // ct
    return pl.pallas_call(
        _mlstm_kernel,
        grid=(batch, nt),
        in_specs=[
            pl.BlockSpec((ct, D_MODEL), lambda b, i: (b * nt + i, 0)),
            _full(ng.shape), _full(wb.shape), _full(wg.shape), _full(cw.shape), _full(cb.shape),
            _full(gbc.shape), _full(gbr.shape), _full(mhg.shape),
        ],
        out_specs=pl.BlockSpec((ct, MLSTM_WIDTH), lambda b, i: (b * nt + i, 0)),
        out_shape=jax.ShapeDtypeStruct((batch * seq, MLSTM_WIDTH), BF16),
        scratch_shapes=[
            pltpu.VMEM((ct + SUBLANES, 2 * MLSTM_WIDTH), F32),
            pltpu.VMEM((MLSTM_HEADS, MLSTM_HEAD_DIM, 2 * MLSTM_HEAD_DIM), F32),
            pltpu.VMEM((SUBLANES, LANES), F32),
        ],
        compiler_params=_params("arbitrary", "arbitrary"),
        name="mlstm",
    )(x2, ng, wb, wg, cw, cb, gbc, gbr, mhg)


def _dsaprep_kernel(x_ref, ng_ref, wc_ref, qg_ref, wuq_ref, wqi_ref, kg_ref, kb_ref,
                    q_ref, qi_ref, k_ref, v_ref, ki_ref, wi_ref, zc_ref):
    tm = x_ref.shape[0]
    hb = _rmsnorm(x_ref[...], ng_ref[...]).astype(BF16)
    pc = _dot(hb, wc_ref[...])
    cq = pc[:, :Q_LORA_RANK]
    o = Q_LORA_RANK
    k_ref[...] = pc[:, o:o + ATTN_HEAD_DIM].astype(BF16)
    v_ref[...] = pc[:, o + ATTN_HEAD_DIM:o + 2 * ATTN_HEAD_DIM].astype(BF16)
    o += 2 * ATTN_HEAD_DIM
    zc_ref[...] = _silu(pc[:, o:o + ATTN_WIDTH]).astype(BF16)
    o += ATTN_WIDTH
    tail = pc[:, o:o + LANES]
    kidx = tail[:, :IDX_HEAD_DIM]
    mu = jnp.mean(kidx, axis=-1, keepdims=True)
    kc = kidx - mu
    ki = kc * lax.rsqrt(jnp.mean(kc * kc, axis=-1, keepdims=True) + EPS) * kg_ref[...] + kb_ref[...]
    ki_ref[...] = ki.astype(BF16)
    wi_ref[...] = tail * ((IDX_HEADS ** -0.5) * (IDX_HEAD_DIM ** -0.5))

    cqn = _rmsnorm(cq, qg_ref[...]).astype(BF16)
    q = (_dot(cqn, wuq_ref[...]) * (ATTN_HEAD_DIM ** -0.5)).astype(BF16)
    qi = _dot(cqn, wqi_ref[...]).astype(BF16)
    for b in range(tm // Q_BLOCK):
        rows = slice(b * Q_BLOCK, (b + 1) * Q_BLOCK)
        for hh in range(ATTN_HEADS):
            q_ref[b, hh * Q_BLOCK:(hh + 1) * Q_BLOCK, :] = q[rows, hh * ATTN_HEAD_DIM:(hh + 1) * ATTN_HEAD_DIM]
        for hh in range(IDX_HEADS):
            qi_ref[b, hh * Q_BLOCK:(hh + 1) * Q_BLOCK, :] = qi[rows, hh * IDX_HEAD_DIM:(hh + 1) * IDX_HEAD_DIM]


def _dsaprep_call(x2, ng, wc, qg, wuq, wqi, kg, kb):
    t = x2.shape[0]
    tm = PREP_TILE
    nq = tm // Q_BLOCK
    row = lambda i: (i, 0)
    return pl.pallas_call(
        _dsaprep_kernel,
        grid=(t // tm,),
        in_specs=[
            pl.BlockSpec((tm, D_MODEL), row),
            _full(ng.shape), _full(wc.shape), _full(qg.shape), _full(wuq.shape), _full(wqi.shape),
            _full(kg.shape), _full(kb.shape),
        ],
        out_specs=[
            pl.BlockSpec((nq, ATTN_HEADS * Q_BLOCK, ATTN_HEAD_DIM), lambda i: (i, 0, 0)),
            pl.BlockSpec((nq, IDX_HEADS * Q_BLOCK, IDX_HEAD_DIM), lambda i: (i, 0, 0)),
            pl.BlockSpec((tm, ATTN_HEAD_DIM), row),
            pl.BlockSpec((tm, ATTN_HEAD_DIM), row),
            pl.BlockSpec((tm, IDX_HEAD_DIM), row),
            pl.BlockSpec((tm, LANES), row),
            pl.BlockSpec((tm, ATTN_WIDTH), row),
        ],
        out_shape=[
            jax.ShapeDtypeStruct((t // Q_BLOCK, ATTN_HEADS * Q_BLOCK, ATTN_HEAD_DIM), BF16),
            jax.ShapeDtypeStruct((t // Q_BLOCK, IDX_HEADS * Q_BLOCK, IDX_HEAD_DIM), BF16),
            jax.ShapeDtypeStruct((t, ATTN_HEAD_DIM), BF16),
            jax.ShapeDtypeStruct((t, ATTN_HEAD_DIM), BF16),
            jax.ShapeDtypeStruct((t, IDX_HEAD_DIM), BF16),
            jax.ShapeDtypeStruct((t, LANES), F32),
            jax.ShapeDtypeStruct((t, ATTN_WIDTH), BF16),
        ],
        compiler_params=_params("arbitrary"),
        name="dsaprep",
    )(x2, ng, wc, qg, wuq, wqi, kg, kb)


def _dsa_kernel(q_ref, qi_ref, wi_ref, zc_ref, ki_ref, k_ref, v_ref, out_ref,
                sc_scr, wb_scr, m_scr, l_scr, acc_scr, *, topk):
    j = pl.program_id(1)
    qb = Q_BLOCK
    n_t = (j * qb + qb + KEY_TILE - 1) // KEY_TILE
    t_pos = j * qb + lax.broadcasted_iota(jnp.int32, (qb, KEY_TILE), 0)
    s_loc = lax.broadcasted_iota(jnp.int32, (qb, KEY_TILE), 1)

    wi = wi_ref[...]
    for hh in range(IDX_HEADS):
        col = IDX_HEAD_DIM + hh
        wb_scr[hh * qb:(hh + 1) * qb, :] = jnp.broadcast_to(wi[:, col:col + 1], (qb, LANES))

    qi = qi_ref[0]

    def score_tile(kt, carry):
        base = pl.multiple_of(kt * KEY_TILE, KEY_TILE)
        z = _dot_nt(qi, ki_ref[pl.ds(base, KEY_TILE), :])
        wb = wb_scr[...]
        cols = []
        for c in range(KEY_TILE // LANES):
            r = jnp.maximum(z[:, c * LANES:(c + 1) * LANES], 0.0) * wb
            cols.append(jnp.sum(r.reshape(IDX_HEADS, qb, LANES), axis=0))
        sc = jnp.concatenate(cols, axis=1)
        sc = jnp.where(base + s_loc <= t_pos, sc, -jnp.inf)
        sc_scr[:, pl.ds(base, KEY_TILE)] = sc
        return carry

    lax.fori_loop(0, n_t, score_tile, 0)

    def row_reduce(fn, init):
        def step(kt, acc):
            base = pl.multiple_of(kt * KEY_TILE, KEY_TILE)
            return fn(acc, sc_scr[:, pl.ds(base, KEY_TILE)])
        return lax.fori_loop(0, n_t, step, init)

    def count_ge(thr):
        def add(acc, sc):
            hit = jnp.where(sc >= thr, 1.0, 0.0)
            for c in range(KEY_TILE // LANES):
                acc = acc + hit[:, c * LANES:(c + 1) * LANES]
            return acc
        acc = row_reduce(add, jnp.zeros((qb, LANES), F32))
        return jnp.sum(acc, axis=-1, keepdims=True)

    def fold_max(acc, sc):
        for c in range(KEY_TILE // LANES):
            acc = jnp.maximum(acc, sc[:, c * LANES:(c + 1) * LANES])
        return acc

    def fold_min(acc, sc):
        for c in range(KEY_TILE // LANES):
            blk = sc[:, c * LANES:(c + 1) * LANES]
            acc = jnp.minimum(acc, jnp.where(blk == -jnp.inf, jnp.inf, blk))
        return acc

    hi0 = jnp.max(row_reduce(fold_max, jnp.full((qb, LANES), -jnp.inf, F32)), axis=-1, keepdims=True)
    lo0 = jnp.min(row_reduce(fold_min, jnp.full((qb, LANES), jnp.inf, F32)), axis=-1, keepdims=True)
    kf = float(topk)
    n_causal = (t_pos[:, 0:1] + 1).astype(F32)
    done0 = n_causal <= kf

    def bis_cond(c):
        it, _, _, done = c
        return jnp.logical_and(it < MAX_BISECT, jnp.min(done) < 0.5)

    def bis_body(c):
        it, lo, hi, done = c
        mid = 0.5 * (lo + hi)
        cnt = count_ge(mid)
        ge = cnt >= kf
        active = done < 0.5
        lo = jnp.where(jnp.logical_and(active, ge), mid, lo)
        hi = jnp.where(jnp.logical_and(active, jnp.logical_not(ge)), mid, hi)
        done = jnp.where(jnp.logical_and(active, cnt == kf), 1.0, done)
        return it + 1, lo, hi, done

    _, thr, _, _ = lax.while_loop(bis_cond, bis_body,
                                  (jnp.int32(0), lo0, hi0, done0.astype(F32)))

    m_scr[...] = jnp.full_like(m_scr, NEG_BIG)
    l_scr[...] = jnp.zeros_like(l_scr)
    acc_scr[...] = jnp.zeros_like(acc_scr)
    q = q_ref[0]

    def attn_tile(kt, carry):
        base = pl.multiple_of(kt * KEY_TILE, KEY_TILE)
        logits = _dot_nt(q, k_ref[pl.ds(base, KEY_TILE), :])
        sel = sc_scr[:, pl.ds(base, KEY_TILE)] >= thr
        sel4 = jnp.concatenate([sel] * ATTN_HEADS, axis=0)
        logits = jnp.where(sel4, logits, NEG_BIG)
        m_old = m_scr[...]
        m_new = jnp.maximum(m_old, jnp.max(logits, axis=-1, keepdims=True))
        alpha = jnp.exp(m_old - m_new)
        p = jnp.exp(logits - m_new[:, 0:1])
        l_scr[...] = alpha * l_scr[...] + jnp.sum(p, axis=-1, keepdims=True)
        acc_scr[...] = alpha * acc_scr[...] + _dot(p.astype(BF16), v_ref[pl.ds(base, KEY_TILE), :])
        m_scr[...] = m_new
        return carry

    lax.fori_loop(0, n_t, attn_tile, 0)
    o = acc_scr[...] / l_scr[...]
    o = jnp.concatenate([o[hh * qb:(hh + 1) * qb, :] for hh in range(ATTN_HEADS)], axis=1)
    out_ref[...] = (o * zc_ref[...].astype(F32)).astype(out_ref.dtype)


def _dsa_call(q_hm, qi_hm, wi, zc, ki, k, v, batch, seq):
    nb = seq // Q_BLOCK
    topk = min(TOPK_MAX, seq // 4)
    blk3 = lambda b, j: (b * nb + j, 0, 0)
    blk2 = lambda b, j: (b * nb + j, 0)
    per_b = lambda b, j: (b, 0)
    return pl.pallas_call(
        functools.partial(_dsa_kernel, topk=topk),
        grid=(batch, nb),
        in_specs=[
            pl.BlockSpec((1, ATTN_HEADS * Q_BLOCK, ATTN_HEAD_DIM), blk3),
            pl.BlockSpec((1, IDX_HEADS * Q_BLOCK, IDX_HEAD_DIM), blk3),
            pl.BlockSpec((Q_BLOCK, LANES), blk2),
            pl.BlockSpec((Q_BLOCK, ATTN_WIDTH), blk2),
            pl.BlockSpec((seq, IDX_HEAD_DIM), per_b),
            pl.BlockSpec((seq, ATTN_HEAD_DIM), per_b),
            pl.BlockSpec((seq, ATTN_HEAD_DIM), per_b),
        ],
        out_specs=pl.BlockSpec((Q_BLOCK, ATTN_WIDTH), blk2),
        out_shape=jax.ShapeDtypeStruct((batch * seq, ATTN_WIDTH), BF16),
        scratch_shapes=[
            pltpu.VMEM((Q_BLOCK, seq), F32),
            pltpu.VMEM((IDX_HEADS * Q_BLOCK, LANES), F32),
            pltpu.VMEM((ATTN_HEADS * Q_BLOCK, LANES), F32),
            pltpu.VMEM((ATTN_HEADS * Q_BLOCK, LANES), F32),
            pltpu.VMEM((ATTN_HEADS * Q_BLOCK, ATTN_HEAD_DIM), F32),
        ],
        compiler_params=_params("arbitrary", "arbitrary"),
        name="dsa",
    )(q_hm, qi_hm, wi, zc, ki, k, v)


def _merge_kernel(x_ref, ng_ref, wg_ref, ya_ref, yb_ref, yc_ref, wbr_ref, wo_ref, fg_ref, out_ref,
                  *, final_norm):
    x = x_ref[...]
    hb = _rmsnorm(x, ng_ref[...]).astype(BF16)
    merged = None
    for n, y_ref in enumerate((ya_ref, yb_ref, yc_ref)):
        gate = jax.nn.sigmoid(_dot(hb, wg_ref[:, n * D_MODEL:(n + 1) * D_MODEL]))
        term = gate * _dot(y_ref[...], wbr_ref[n])
        merged = term if merged is None else merged + term
    y = x + _dot(merged.astype(BF16), wo_ref[...])
    if final_norm:
        y = _rmsnorm(y, fg_ref[...])
    out_ref[...] = y


def _merge_call(x2, ng, wg, ya, yb, yc, wbr, wo, fg, final_norm):
    t = x2.shape[0]
    tm = MERGE_TILE
    row = lambda i: (i, 0)
    return pl.pallas_call(
        functools.partial(_merge_kernel, final_norm=final_norm),
        grid=(t // tm,),
        in_specs=[
            pl.BlockSpec((tm, D_MODEL), row),
            _full(ng.shape), _full(wg.shape),
            pl.BlockSpec((tm, SSM_WIDTH), row),
            pl.BlockSpec((tm, MLSTM_WIDTH), row),
            pl.BlockSpec((tm, ATTN_WIDTH), row),
            _full(wbr.shape), _full(wo.shape), _full(fg.shape),
        ],
        out_specs=pl.BlockSpec((tm, D_MODEL), row),
        out_shape=jax.ShapeDtypeStruct((t, D_MODEL), F32),
        compiler_params=_params("arbitrary"),
        name="merge",
    )(x2, ng, wg, ya, yb, yc, wbr, wo, fg)


def _row(v):
    return v.astype(F32).reshape(1, -1)


def _layer(x2, batch, seq, final_norm, norm_g, w_in, ssm_a_re, ssm_a_im, ssm_b_re, ssm_b_im,
           ssm_c_re, ssm_c_im, ssm_log_dt, ssm_d, glu_w, glu_b, qk_conv_w, qk_conv_b, igate_b,
           fgate_b, mh_norm_g, q_norm_g, w_uq, w_qidx, kidx_norm_g, kidx_norm_b, w_branch, w_out,
           final_norm_g):
    ng = _row(norm_g)
    o_a = 0
    o_b = o_a + 2 * SSM_WIDTH
    o_if = o_b + 3 * MLSTM_WIDTH
    o_zb = o_if + 2 * MLSTM_HEADS
    o_c = o_zb + MLSTM_WIDTH
    o_kidx = o_c + Q_LORA_RANK + 2 * ATTN_HEAD_DIM
    o_zc = o_kidx + IDX_HEAD_DIM + IDX_HEADS
    o_g = o_zc + ATTN_WIDTH

    wa = w_in[:, o_a:o_b].astype(BF16)
    wbu, cst, wc5 = _s5_tables(ssm_a_re, ssm_a_im, ssm_b_re, ssm_b_im, ssm_c_re, ssm_c_im, ssm_log_dt)
    ya = _s5_call(x2, ng, wa, wbu, cst, wc5, _row(ssm_d), glu_w.astype(BF16), _row(glu_b), batch, seq)

    wb = jnp.concatenate([w_in[:, o_b:o_if], w_in[:, o_zb:o_c]], axis=1).astype(BF16)
    wgt = jnp.pad(w_in[:, o_if:o_zb].astype(F32), ((0, 0), (0, LANES - 2 * MLSTM_HEADS)))
    gbias = jnp.concatenate([igate_b.astype(F32), fgate_b.astype(F32)])
    gbc = jnp.pad(gbias, (0, LANES - 2 * MLSTM_HEADS)).reshape(1, LANES)
    gbr = gbias.reshape(2 * MLSTM_HEADS, 1)
    yb = _mlstm_call(x2, ng, wb, wgt, qk_conv_w.astype(F32), _row(qk_conv_b), gbc, gbr,
                     _row(mh_norm_g), batch, seq)

    tail = jnp.pad(w_in[:, o_kidx:o_zc], ((0, 0), (0, LANES - IDX_HEAD_DIM - IDX_HEADS)))
    wcd = jnp.concatenate([w_in[:, o_c:o_kidx], w_in[:, o_zc:o_g], tail], axis=1).astype(BF16)
    q_hm, qi_hm, k, v, ki, wi, zc = _dsaprep_call(
        x2, ng, wcd, _row(q_norm_g), w_uq.astype(BF16), w_qidx.astype(BF16),
        _row(kidx_norm_g), _row(kidx_norm_b))
    yc = _dsa_call(q_hm, qi_hm, wi, zc, ki, k, v, batch, seq)

    wgm = w_in[:, o_g:].astype(BF16)
    return _merge_call(x2, ng, wgm, ya, yb, yc, w_branch.astype(BF16), w_out.astype(BF16),
                       _row(final_norm_g), final_norm)


def kernel(x, norm_g, w_in, ssm_a_re, ssm_a_im, ssm_b_re, ssm_b_im, ssm_c_re, ssm_c_im, ssm_log_dt, ssm_d, glu_w, glu_b, qk_conv_w, qk_conv_b, igate_b, fgate_b, mh_norm_g, q_norm_g, w_uq, w_qidx, kidx_norm_g, kidx_norm_b, w_branch, w_out, final_norm_g):
    batch, seq, d = x.shape
    depth = norm_g.shape[0]
    x2 = x.astype(F32).reshape(batch * seq, d)
    stacked = (norm_g, w_in, ssm_a_re, ssm_a_im, ssm_b_re, ssm_b_im, ssm_c_re, ssm_c_im, ssm_log_dt,
               ssm_d, glu_w, glu_b, qk_conv_w, qk_conv_b, igate_b, fgate_b, mh_norm_g, q_norm_g, w_uq,
               w_qidx, kidx_norm_g, kidx_norm_b, w_branch, w_out)
    for l in range(depth):
        x2 = _layer(x2, batch, seq, l == depth - 1, *[p[l] for p in stacked], final_norm_g)
    return x2.reshape(batch, seq, d).astype(x.dtype)
```

```python
import functools
import math

import jax
import jax.numpy as jnp
from jax import lax
from jax.experimental import pallas as pl
from jax.experimental.pallas import tpu as pltpu

F32 = jnp.float32
BF16 = jnp.bfloat16

D_MODEL = 1024
EPS = 1e-6
SSM_WIDTH = 512
SSM_GROUP = 16
SSM_GROUPS = 32
SSM_STATE = 64
SSM_LANES = SSM_GROUPS * SSM_STATE
SSM_BLOCKS = 4
MLSTM_HEADS = 4
MLSTM_HEAD_DIM = 128
MLSTM_WIDTH = 512
CONV_WIDTH = 4
ATTN_HEADS = 4
ATTN_HEAD_DIM = 128
ATTN_WIDTH = 512
Q_LORA_RANK = 256
IDX_HEADS = 8
IDX_HEAD_DIM = 64
TOPK_MAX = 256
Q_BLOCK = 128
N_BRANCH = 3

SUBLANES = 8
LANES = 128
S5_TILE = 256
MLSTM_TILE = 256
PREP_TILE = 256
MERGE_TILE = 256
KEY_TILE = 512
NEG_BIG = -1e30
LOG2E = math.log2(math.e)
VMEM_LIMIT = 56 * 1024 * 1024
MAX_BISECT = 48
BLIND_BISECT = 8
STALL_ITERS = 3.0


def _dot(a, b):
    return jnp.dot(a, b, preferred_element_type=F32)


def _dot_nt(a, b):
    return lax.dot_general(a, b, (((1,), (1,)), ((), ())), preferred_element_type=F32)


def _dot_f32(a, b):
    return jnp.dot(a, b, preferred_element_type=F32, precision=lax.Precision.HIGHEST)


def _tree(fn, xs):
    xs = list(xs)
    while len(xs) > 1:
        xs = [fn(xs[i], xs[i + 1]) if i + 1 < len(xs) else xs[i] for i in range(0, len(xs), 2)]
    return xs[0]


def _rmsnorm(x, g):
    return x * lax.rsqrt(jnp.mean(x * x, axis=-1, keepdims=True) + EPS) * g


def _silu(x):
    return x * jax.nn.sigmoid(x)


def _params(*sem):
    return pltpu.CompilerParams(dimension_semantics=sem, vmem_limit_bytes=VMEM_LIMIT)


def _full(shape):
    n = len(shape)
    return pl.BlockSpec(shape, lambda *_: (0,) * n)


def _s5_kernel(x_ref, ng_ref, wa_ref, wbu_ref, cst_ref, wc_ref, d_ref, gw_ref, gb_ref,
               out_ref, bre, bim, car):
    tl = x_ref.shape[0]

    @pl.when(pl.program_id(1) == 0)
    def _():
        car[...] = jnp.zeros_like(car)

    h = _rmsnorm(x_ref[...], ng_ref[...]).astype(BF16)
    pa = _dot(h, wa_ref[...])
    u = pa[:, :SSM_WIDTH]
    z = pa[:, SSM_WIDTH:]
    ub = u.astype(BF16)
    half = SSM_LANES // SSM_BLOCKS
    for k in range(SSM_BLOCKS):
        blk = _dot(ub[:, k * LANES:(k + 1) * LANES], wbu_ref[k])
        bre[:, k * half:(k + 1) * half] = blk[:, :half]
        bim[:, k * half:(k + 1) * half] = blk[:, half:]

    def body(r, c):
        cr, ci = c
        rows = pl.ds(pl.multiple_of(r * SUBLANES, SUBLANES), SUBLANES)
        xr = bre[rows, :]
        xi = bim[rows, :]
        for lvl in range(3):
            ar = cst_ref[2 * lvl]
            ai = cst_ref[2 * lvl + 1]
            sr = pltpu.roll(xr, 1 << lvl, 0)
            si = pltpu.roll(xi, 1 << lvl, 0)
            xr, xi = xr + ar * sr - ai * si, xi + ar * si + ai * sr
        pr = cst_ref[6]
        pi = cst_ref[7]
        hr = xr + pr * cr - pi * ci
        hi = xi + pr * ci + pi * cr
        bre[rows, :] = hr
        bim[rows, :] = hi
        last = SUBLANES - 1
        return (jnp.broadcast_to(hr[last:, :], hr.shape), jnp.broadcast_to(hi[last:, :], hi.shape))

    cr, ci = lax.fori_loop(0, tl // SUBLANES, body, (car[0], car[1]))
    car[0] = cr
    car[1] = ci

    hre = bre[...].astype(BF16)
    him = bim[...].astype(BF16)
    ys = []
    for k in range(SSM_BLOCKS):
        hk = jnp.concatenate([hre[:, k * half:(k + 1) * half], him[:, k * half:(k + 1) * half]], axis=1)
        ys.append(_dot(hk, wc_ref[k]))
    y = jnp.concatenate(ys, axis=1) + d_ref[...] * u
    g = jax.nn.gelu(y)
    out = g * jax.nn.sigmoid(_dot(g.astype(BF16), gw_ref[...]) + gb_ref[...])
    out_ref[...] = (out * _silu(z)).astype(out_ref.dtype)


def _s5_tables(a_re, a_im, b_re, b_im, c_re, c_im, log_dt):
    dt = jnp.exp(log_dt.astype(F32))[:, None]
    ar = a_re.astype(F32)
    ai = a_im.astype(F32)

    def apow(k):
        mag = jnp.exp(k * ar * dt)
        ang = k * ai * dt
        return (mag * jnp.cos(ang)).reshape(-1), (mag * jnp.sin(ang)).reshape(-1)

    abr, abi = (mag.reshape(ar.shape) for mag in apow(1))
    den = ar * ar + ai * ai
    fr = ((abr - 1.0) * ar + abi * ai) / den
    fi = (abi * ar - (abr - 1.0) * ai) / den
    br = b_re.astype(F32)
    bi = b_im.astype(F32)
    bb_re = fr[..., None] * br - fi[..., None] * bi
    bb_im = fr[..., None] * bi + fi[..., None] * br
    gpb = SSM_GROUPS // SSM_BLOCKS
    eye = jnp.eye(gpb, dtype=F32)

    def bu_block(bb):
        bb4 = bb.reshape(SSM_BLOCKS, gpb, SSM_STATE, SSM_GROUP)
        w = jnp.einsum('kgnp,gh->kgphn', bb4, eye)
        return w.reshape(SSM_BLOCKS, gpb * SSM_GROUP, gpb * SSM_STATE)

    wbu = jnp.concatenate([bu_block(bb_re), bu_block(bb_im)], axis=2).astype(BF16)

    def c_block(c):
        c4 = c.astype(F32).reshape(SSM_BLOCKS, gpb, SSM_GROUP, SSM_STATE)
        w = jnp.einsum('kgpn,gh->kgnhp', c4, eye)
        return w.reshape(SSM_BLOCKS, gpb * SSM_STATE, gpb * SSM_GROUP)

    wc = jnp.concatenate([c_block(c_re), -c_block(c_im)], axis=1).astype(BF16)

    row = jnp.arange(SUBLANES, dtype=F32)[:, None]
    tabs = []
    for lvl in range(3):
        k = 1 << lvl
        pr, pi = apow(float(k))
        keep = row >= k
        tabs += [jnp.where(keep, pr[None, :], 0.0), jnp.where(keep, pi[None, :], 0.0)]
    prs, pis = zip(*[apow(float(r + 1)) for r in range(SUBLANES)])
    tabs += [jnp.stack(prs), jnp.stack(pis)]
    cst = jnp.stack(tabs).astype(F32)
    return wbu, cst, wc


def _s5_call(x2, ng, wa, wbu, cst, wc, dskip, gw, gb, batch, seq):
    tl = S5_TILE
    nt = seq // tl
    return pl.pallas_call(
        _s5_kernel,
        grid=(batch, nt),
        in_specs=[
            pl.BlockSpec((tl, D_MODEL), lambda b, i: (b * nt + i, 0)),
            _full(ng.shape), _full(wa.shape), _full(wbu.shape), _full(cst.shape), _full(wc.shape),
            _full(dskip.shape), _full(gw.shape), _full(gb.shape),
        ],
        out_specs=pl.BlockSpec((tl, SSM_WIDTH), lambda b, i: (b * nt + i, 0)),
        out_shape=jax.ShapeDtypeStruct((batch * seq, SSM_WIDTH), BF16),
        scratch_shapes=[
            pltpu.VMEM((tl, SSM_LANES), F32),
            pltpu.VMEM((tl, SSM_LANES), F32),
            pltpu.VMEM((2, SUBLANES, SSM_LANES), F32),
        ],
        compiler_params=_params("arbitrary", "arbitrary"),
        name="s5",
    )(x2, ng, wa, wbu, cst, wc, dskip, gw, gb)


def _mlstm_kernel(x_ref, ng_ref, wb_ref, wg_ref, cw_ref, cb_ref, gbc_ref, gbr_ref, mhg_ref,
                  out_ref, qk_scr, st_scr, m_scr):
    ct = x_ref.shape[0]
    hd = MLSTM_HEAD_DIM

    @pl.when(pl.program_id(1) == 0)
    def _():
        qk_scr[...] = jnp.zeros_like(qk_scr)
        st_scr[...] = jnp.zeros_like(st_scr)
        m_scr[...] = jnp.zeros_like(m_scr)

    h32 = _rmsnorm(x_ref[...], ng_ref[...])
    hb = h32.astype(BF16)
    pb = _dot(hb, wb_ref[...])
    gates = _dot_f32(h32, wg_ref[...])

    qk_scr[0:SUBLANES, :] = qk_scr[ct:ct + SUBLANES, :]
    qk_scr[SUBLANES:, :] = pb[:, :2 * MLSTM_WIDTH]
    conv = cb_ref[...]
    for j in range(CONV_WIDTH):
        off = SUBLANES - (CONV_WIDTH - 1) + j
        conv = conv + cw_ref[j:j + 1, :] * qk_scr[off:off + ct, :]
    qk = _silu(conv)
    q = qk[:, :MLSTM_WIDTH].astype(BF16)
    k = (qk[:, MLSTM_WIDTH:] * (hd ** -0.5)).astype(BF16)
    v = pb[:, 2 * MLSTM_WIDTH:3 * MLSTM_WIDTH].astype(BF16)
    z = pb[:, 3 * MLSTM_WIDTH:]

    gc = gates + gbc_ref[...]
    gr = gates.T[:SUBLANES, :] + gbr_ref[...]

    def logsig(t):
        return jnp.minimum(t, 0.0) - jnp.log1p(jnp.exp(-jnp.abs(t)))

    ti = lax.broadcasted_iota(jnp.int32, (ct, ct), 0)
    si = lax.broadcasted_iota(jnp.int32, (ct, ct), 1)
    causal = si <= ti
    lower = causal.astype(F32)
    upper = (ti <= si).astype(F32)
    bcum_c = _dot_f32(lower, logsig(gc))
    bcum_r = _dot_f32(logsig(gr), upper)
    ones_v = jnp.ones((ct, hd), BF16)

    outs = []
    for hh in range(MLSTM_HEADS):
        lo = hh * hd
        a_c = bcum_c[:, MLSTM_HEADS + hh:MLSTM_HEADS + hh + 1]
        i_c = gc[:, hh:hh + 1]
        a_r = bcum_r[MLSTM_HEADS + hh:MLSTM_HEADS + hh + 1, :]
        i_r = gr[hh:hh + 1, :]
        m_prev = m_scr[hh:hh + 1, 0:1]
        b_last = a_c[ct - 1:ct, :]

        dmat = jnp.where(causal, a_c - a_r + i_r, -jnp.inf)
        inter = a_c + m_prev
        m = jnp.maximum(inter, jnp.max(dmat, axis=-1, keepdims=True))
        dexp = jnp.exp(dmat - m)
        inter_w = jnp.exp(inter - m)

        qh = q[:, lo:lo + hd]
        kh = k[:, lo:lo + hd]
        v_aug = jnp.concatenate([v[:, lo:lo + hd], ones_v], axis=1)
        s = (_dot_nt(qh, kh) * dexp).astype(BF16)
        st = st_scr[hh]
        numden = _dot(s, v_aug) + inter_w * _dot(qh, st.astype(BF16))
        num = numden[:, :hd]
        den = numden[:, hd:hd + 1]
        hout = num / jnp.maximum(jnp.abs(den), jnp.exp(-m))

        g_end = b_last - a_c + i_c
        m_next = jnp.maximum(b_last + m_prev, jnp.max(b_last - a_r + i_r, axis=-1, keepdims=True))
        w_s = jnp.exp(g_end - m_next)
        decay = jnp.exp(b_last + m_prev - m_next)
        kw = (kh.astype(F32) * w_s).T.astype(BF16)
        st_scr[hh] = decay * st + _dot(kw, v_aug)
        m_scr[hh:hh + 1, :] = jnp.broadcast_to(m_next, (1, LANES))

        hn = hout * lax.rsqrt(jnp.mean(hout * hout, axis=-1, keepdims=True) + EPS)
        outs.append(hn)
    hcat = jnp.concatenate(outs, axis=1) * mhg_ref[...]
    out_ref[...] = (hcat * _silu(z)).astype(out_ref.dtype)


def _mlstm_call(x2, ng, wb, wg, cw, cb, gbc, gbr, mhg, batch, seq):
    ct = MLSTM_TILE
    nt = seq // ct
    return pl.pallas_call(
        _mlstm_kernel,
        grid=(batch, nt),
        in_specs=[
            pl.BlockSpec((ct, D_MODEL), lambda b, i: (b * nt + i, 0)),
            _full(ng.shape), _full(wb.shape), _full(wg.shape), _full(cw.shape), _full(cb.shape),
            _full(gbc.shape), _full(gbr.shape), _full(mhg.shape),
        ],
        out_specs=pl.BlockSpec((ct, MLSTM_WIDTH), lambda b, i: (b * nt + i, 0)),
        out_shape=jax.ShapeDtypeStruct((batch * seq, MLSTM_WIDTH), BF16),
        scratch_shapes=[
            pltpu.VMEM((ct + SUBLANES, 2 * MLSTM_WIDTH), F32),
            pltpu.VMEM((MLSTM_HEADS, MLSTM_HEAD_DIM, 2 * MLSTM_HEAD_DIM), F32),
            pltpu.VMEM((SUBLANES, LANES), F32),
        ],
        compiler_params=_params("arbitrary", "arbitrary"),
        name="mlstm",
    )(x2, ng, wb, wg, cw, cb, gbc, gbr, mhg)


def _dsaprep_kernel(x_ref, ng_ref, wc_ref, qg_ref, wuq_ref, wqi_ref, kg_ref, kb_ref,
                    qh_ref, qit_ref, wt_ref, k_ref, v_ref, ki_ref, zc_ref):
    tm = x_ref.shape[0]
    hb = _rmsnorm(x_ref[...], ng_ref[...]).astype(BF16)
    pc = _dot(hb, wc_ref[...])
    cq = pc[:, :Q_LORA_RANK]
    o = Q_LORA_RANK
    k_ref[...] = pc[:, o:o + ATTN_HEAD_DIM].astype(BF16)
    v_ref[...] = pc[:, o + ATTN_HEAD_DIM:o + 2 * ATTN_HEAD_DIM].astype(BF16)
    o += 2 * ATTN_HEAD_DIM
    zc_ref[...] = _silu(pc[:, o:o + ATTN_WIDTH]).astype(BF16)
    o += ATTN_WIDTH
    tail = pc[:, o:o + LANES]
    kidx = tail[:, :IDX_HEAD_DIM]
    mu = jnp.mean(kidx, axis=-1, keepdims=True)
    kc = kidx - mu
    ki = kc * lax.rsqrt(jnp.mean(kc * kc, axis=-1, keepdims=True) + EPS) * kg_ref[...] + kb_ref[...]
    ki_ref[...] = ki.astype(BF16)
    wi = tail * ((IDX_HEADS ** -0.5) * (IDX_HEAD_DIM ** -0.5))

    cqn = _rmsnorm(cq, qg_ref[...]).astype(BF16)
    q = (_dot(cqn, wuq_ref[...]) * (ATTN_HEAD_DIM ** -0.5 * LOG2E)).astype(BF16)
    qi = _dot(cqn, wqi_ref[...])
    for b in range(tm // Q_BLOCK):
        rows = slice(b * Q_BLOCK, (b + 1) * Q_BLOCK)
        for hh in range(ATTN_HEADS):
            qh_ref[b, hh * Q_BLOCK:(hh + 1) * Q_BLOCK, :] = q[rows, hh * ATTN_HEAD_DIM:(hh + 1) * ATTN_HEAD_DIM]
        qit = qi[rows, :].T
        qit_ref[b] = jnp.concatenate(
            [qit[hh * IDX_HEAD_DIM:(hh + 1) * IDX_HEAD_DIM, :] for hh in range(IDX_HEADS)], axis=1).astype(BF16)
        wt_ref[b] = wi[rows, :].T[IDX_HEAD_DIM:IDX_HEAD_DIM + IDX_HEADS, :]


def _dsaprep_call(x2, ng, wc, qg, wuq, wqi, kg, kb):
    t = x2.shape[0]
    tm = PREP_TILE
    nq = tm // Q_BLOCK
    row = lambda i: (i, 0)
    blk = lambda i: (i, 0, 0)
    return pl.pallas_call(
        _dsaprep_kernel,
        grid=(t // tm,),
        in_specs=[
            pl.BlockSpec((tm, D_MODEL), row),
            _full(ng.shape), _full(wc.shape), _full(qg.shape), _full(wuq.shape), _full(wqi.shape),
            _full(kg.shape), _full(kb.shape),
        ],
        out_specs=[
            pl.BlockSpec((nq, ATTN_HEADS * Q_BLOCK, ATTN_HEAD_DIM), blk),
            pl.BlockSpec((nq, IDX_HEAD_DIM, IDX_HEADS * Q_BLOCK), blk),
            pl.BlockSpec((nq, IDX_HEADS, Q_BLOCK), blk),
            pl.BlockSpec((tm, ATTN_HEAD_DIM), row),
            pl.BlockSpec((tm, ATTN_HEAD_DIM), row),
            pl.BlockSpec((tm, IDX_HEAD_DIM), row),
            pl.BlockSpec((tm, ATTN_WIDTH), row),
        ],
        out_shape=[
            jax.ShapeDtypeStruct((t // Q_BLOCK, ATTN_HEADS * Q_BLOCK, ATTN_HEAD_DIM), BF16),
            jax.ShapeDtypeStruct((t // Q_BLOCK, IDX_HEAD_DIM, IDX_HEADS * Q_BLOCK), BF16),
            jax.ShapeDtypeStruct((t // Q_BLOCK, IDX_HEADS, Q_BLOCK), F32),
            jax.ShapeDtypeStruct((t, ATTN_HEAD_DIM), BF16),
            jax.ShapeDtypeStruct((t, ATTN_HEAD_DIM), BF16),
            jax.ShapeDtypeStruct((t, IDX_HEAD_DIM), BF16),
            jax.ShapeDtypeStruct((t, ATTN_WIDTH), BF16),
        ],
        compiler_params=_params("arbitrary"),
        name="dsaprep",
    )(x2, ng, wc, qg, wuq, wqi, kg, kb)


def _dsa_kernel(qh_ref, qit_ref, wt_ref, zc_ref, ki_ref, k_ref, v_ref, out_ref,
                sc_scr, sct_scr, lg_scr, m_scr, acc_scr, *, topk, seq):
    j = pl.program_id(1)
    qb = Q_BLOCK
    kt_rows = KEY_TILE
    groups = kt_rows // SUBLANES
    n_t = (j * qb + qb + kt_rows - 1) // kt_rows
    t_pos = j * qb + lax.broadcasted_iota(jnp.int32, (kt_rows, qb), 1)
    s_loc = lax.broadcasted_iota(jnp.int32, (kt_rows, qb), 0)
    kf = float(topk)

    def tile_rows(kt):
        if isinstance(kt, int):
            return pl.ds(kt * kt_rows, kt_rows)
        return pl.ds(pl.multiple_of(kt * kt_rows, kt_rows), kt_rows)

    def chunk_lanes(kt, c):
        if isinstance(kt, int):
            return pl.ds(kt * kt_rows + c * LANES, LANES)
        return pl.ds(pl.multiple_of(kt * kt_rows + c * LANES, LANES), LANES)

    def fold(fn, x):
        return _tree(fn, [x[g * SUBLANES:(g + 1) * SUBLANES, :] for g in range(groups)])

    qit = qit_ref[0]
    wt = wt_ref[0]

    def score_tile(kt, carry):
        mx, mn = carry
        rows = tile_rows(kt)
        z = _dot(ki_ref[rows, :], qit)
        sc = None
        for hh in range(IDX_HEADS):
            r = jnp.maximum(z[:, hh * qb:(hh + 1) * qb], 0.0) * wt[hh:hh + 1, :]
            sc = r if sc is None else sc + r
        causal = kt * kt_rows + s_loc <= t_pos
        masked = jnp.where(causal, sc, -jnp.inf)
        sc_scr[rows, :] = masked
        sct_scr[:, tile_rows(kt)] = masked.T
        mx = jnp.maximum(mx, fold(jnp.maximum, masked))
        mn = jnp.minimum(mn, fold(jnp.minimum, jnp.where(causal, sc, jnp.inf)))
        lg_scr[:, tile_rows(kt)] = _dot_nt(qh_ref[0], k_ref[rows, :])
        return mx, mn

    mx8, mn8 = lax.fori_loop(0, n_t, score_tile,
                             (jnp.full((SUBLANES, qb), -jnp.inf, F32), jnp.full((SUBLANES, qb), jnp.inf, F32)))
    row_max = jnp.max(mx8, axis=0, keepdims=True)
    row_min = jnp.min(mn8, axis=0, keepdims=True)

    def count_ge(thr):
        thr8 = jnp.broadcast_to(thr, (SUBLANES, qb))

        def step(kt, acc):
            sc = sc_scr[tile_rows(kt), :]
            hit = [jnp.where(sc[g * SUBLANES:(g + 1) * SUBLANES, :] >= thr8, 1.0, 0.0) for g in range(groups)]
            return acc + _tree(jnp.add, hit)

        acc = lax.fori_loop(0, n_t, step, jnp.zeros((SUBLANES, qb), F32))
        return jnp.sum(acc, axis=0, keepdims=True)

    def max_below(bound):
        def step(kt, acc):
            sc = sc_scr[tile_rows(kt), :]
            return jnp.maximum(acc, fold(jnp.maximum, jnp.where(sc < bound, sc, -jnp.inf)))

        acc = lax.fori_loop(0, n_t, step, jnp.full((SUBLANES, qb), -jnp.inf, F32))
        return jnp.max(acc, axis=0, keepdims=True)

    def flag(cond):
        return jnp.where(cond, 1.0, 0.0)

    n_causal = (t_pos[0:1, :] + 1).astype(F32)
    lo0 = row_min
    hi0 = row_max + (jnp.abs(row_max) + 1.0) * 1e-6
    done0 = flag(n_causal <= kf)
    zero = jnp.zeros((1, qb), F32)

    def bis_cond(c):
        it, _, _, _, _, done, stall, _ = c
        return jnp.logical_and(it < MAX_BISECT, jnp.min(jnp.maximum(done, stall)) < 0.5)

    def bis_body(c):
        it, lo, hi, clo, chi, done, stall, same = c
        mid = 0.5 * lo + 0.5 * hi
        inside = jnp.logical_and(mid > lo, mid < hi)
        cnt = count_ge(mid)
        act = jnp.logical_and(jnp.maximum(done, stall) < 0.5, inside)
        up = jnp.logical_and(act, cnt >= kf)
        dn = jnp.logical_and(act, cnt < kf)
        nlo = jnp.where(up, mid, lo)
        nclo = jnp.where(up, cnt, clo)
        nhi = jnp.where(dn, mid, hi)
        nchi = jnp.where(dn, cnt, chi)
        same = jnp.where((nclo - nchi) == (clo - chi), same + 1.0, 0.0)
        done = jnp.maximum(done, flag(nclo == kf))
        stall = jnp.maximum(stall, flag(jnp.logical_or(jnp.logical_not(inside), same >= STALL_ITERS)))
        return it + 1, nlo, nhi, nclo, nchi, done, stall, same

    state = (jnp.int32(0), lo0, hi0, n_causal, zero, done0, zero, zero)
    state = lax.fori_loop(0, BLIND_BISECT, lambda _, c: bis_body(c), state)
    _, lo, hi, clo, chi, done, _, _ = lax.while_loop(bis_cond, lambda c: bis_body(bis_body(c)), state)

    def walk_cond(c):
        it, _, _, _, _, done, _ = c
        return jnp.logical_and(it < seq, jnp.min(done) < 0.5)

    def walk_body(c):
        it, lo, hi, clo, chi, done, tie = c
        cand = max_below(hi)
        cge = count_ge(cand)
        act = done < 0.5
        fin = jnp.logical_and(act, cge >= kf)
        mv = jnp.logical_and(act, cge < kf)
        lo = jnp.where(fin, cand, lo)
        clo = jnp.where(fin, cge, clo)
        tie = jnp.maximum(tie, flag(jnp.logical_and(fin, cge > kf)))
        hi = jnp.where(mv, cand, hi)
        chi = jnp.where(mv, cge, chi)
        done = jnp.maximum(done, flag(fin))
        return it + 1, lo, hi, clo, chi, done, tie

    _, thr, _, _, chi, _, tie = lax.while_loop(
        walk_cond, walk_body, (jnp.int32(0), lo, hi, clo, chi, done, zero))
    thr8 = jnp.broadcast_to(thr, (SUBLANES, qb))

    @pl.when(jnp.max(tie) > 0.5)
    def _():
        need = jnp.where(tie > 0.5, kf - chi, float(seq))
        lower = (lax.broadcasted_iota(jnp.int32, (kt_rows, kt_rows), 1)
                 <= lax.broadcasted_iota(jnp.int32, (kt_rows, kt_rows), 0))
        lower = jnp.where(lower, 1.0, 0.0).astype(BF16)

        def step(kt, seen):
            rows = tile_rows(kt)
            sc = sc_scr[rows, :]
            eq = sc == thr
            rank = _dot(lower, jnp.where(eq, 1.0, 0.0).astype(BF16)) + seen
            kept = jnp.where(jnp.logical_and(eq, rank > need), -jnp.inf, sc)
            sc_scr[rows, :] = kept
            sct_scr[:, tile_rows(kt)] = kept.T
            return rank[kt_rows - 1:kt_rows, :]

        lax.fori_loop(0, n_t, step, zero)

    chunks = kt_rows // LANES
    thr_col = jnp.broadcast_to(thr, (qb, qb)).T

    m_scr[...] = jnp.full_like(m_scr, NEG_BIG)

    def max_tile(kt, carry):
        bias = []
        for c in range(chunks):
            lanes = chunk_lanes(kt, c)
            b = jnp.where(sct_scr[:, lanes] >= thr_col, 0.0, NEG_BIG)
            sct_scr[:, lanes] = b
            bias.append(b)
        for hh in range(ATTN_HEADS):
            hrows = slice(hh * qb, (hh + 1) * qb)
            m = m_scr[hrows, :]
            for c in range(chunks):
                m = jnp.maximum(m, lg_scr[hrows, chunk_lanes(kt, c)] + bias[c])
            m_scr[hrows, :] = m
        return carry

    lax.fori_loop(0, n_t, max_tile, 0)
    m_scr[...] = jnp.broadcast_to(jnp.max(m_scr[...], axis=1, keepdims=True), m_scr.shape)

    acc_scr[...] = jnp.zeros_like(acc_scr)
    ones_blk = jnp.ones((kt_rows, ATTN_HEAD_DIM), BF16)

    def attn_tile(kt, carry):
        v_aug = jnp.concatenate([v_ref[tile_rows(kt), :], ones_blk], axis=1)
        for hh in range(ATTN_HEADS):
            hrows = slice(hh * qb, (hh + 1) * qb)
            m = m_scr[hrows, :]
            p = [jnp.exp2(lg_scr[hrows, chunk_lanes(kt, c)] + sct_scr[:, chunk_lanes(kt, c)] - m).astype(BF16)
                 for c in range(chunks)]
            acc_scr[hrows, :] += _dot(jnp.concatenate(p, axis=1), v_aug)
        return carry

    lax.fori_loop(0, n_t, attn_tile, 0)
    acc = acc_scr[...]
    o = jnp.concatenate(
        [acc[hh * qb:(hh + 1) * qb, :ATTN_HEAD_DIM] / acc[hh * qb:(hh + 1) * qb, ATTN_HEAD_DIM:]
         for hh in range(ATTN_HEADS)], axis=1)
    out_ref[...] = (o * zc_ref[...].astype(F32)).astype(out_ref.dtype)


def _dsa_call(qh, qit, wt, zc, ki, k, v, batch, seq):
    nb = seq // Q_BLOCK
    topk = min(TOPK_MAX, seq // 4)
    blk3 = lambda b, j: (b * nb + j, 0, 0)
    blk2 = lambda b, j: (b * nb + j, 0)
    per_b = lambda b, j: (b, 0)
    return pl.pallas_call(
        functools.partial(_dsa_kernel, topk=topk, seq=seq),
        grid=(batch, nb),
        in_specs=[
            pl.BlockSpec((1, ATTN_HEADS * Q_BLOCK, ATTN_HEAD_DIM), blk3),
            pl.BlockSpec((1, IDX_HEAD_DIM, IDX_HEADS * Q_BLOCK), blk3),
            pl.BlockSpec((1, IDX_HEADS, Q_BLOCK), blk3),
            pl.BlockSpec((Q_BLOCK, ATTN_WIDTH), blk2),
            pl.BlockSpec((seq, IDX_HEAD_DIM), per_b),
            pl.BlockSpec((seq, ATTN_HEAD_DIM), per_b),
            pl.BlockSpec((seq, ATTN_HEAD_DIM), per_b),
        ],
        out_specs=pl.BlockSpec((Q_BLOCK, ATTN_WIDTH), blk2),
        out_shape=jax.ShapeDtypeStruct((batch * seq, ATTN_WIDTH), BF16),
        scratch_shapes=[
            pltpu.VMEM((seq, Q_BLOCK), F32),
            pltpu.VMEM((Q_BLOCK, seq + LANES), F32),
            pltpu.VMEM((ATTN_HEADS * Q_BLOCK, seq + LANES), F32),
            pltpu.VMEM((ATTN_HEADS * Q_BLOCK, LANES), F32),
            pltpu.VMEM((ATTN_HEADS * Q_BLOCK, 2 * ATTN_HEAD_DIM), F32),
        ],
        compiler_params=_params("arbitrary", "arbitrary"),
        name="dsa",
    )(qh, qit, wt, zc, ki, k, v)


def _merge_kernel(x_ref, ng_ref, wg_ref, ya_ref, yb_ref, yc_ref, wbr_ref, wo_ref, fg_ref, out_ref,
                  *, final_norm):
    x = x_ref[...]
    hb = _rmsnorm(x, ng_ref[...]).astype(BF16)
    merged = None
    for n, y_ref in enumerate((ya_ref, yb_ref, yc_ref)):
        gate = jax.nn.sigmoid(_dot(hb, wg_ref[:, n * D_MODEL:(n + 1) * D_MODEL]))
        term = gate * _dot(y_ref[...], wbr_ref[n])
        merged = term if merged is None else merged + term
    y = x + _dot(merged.astype(BF16), wo_ref[...])
    if final_norm:
        y = _rmsnorm(y, fg_ref[...])
    out_ref[...] = y


def _merge_call(x2, ng, wg, ya, yb, yc, wbr, wo, fg, final_norm):
    t = x2.shape[0]
    tm = MERGE_TILE
    row = lambda i: (i, 0)
    return pl.pallas_call(
        functools.partial(_merge_kernel, final_norm=final_norm),
        grid=(t // tm,),
        in_specs=[
            pl.BlockSpec((tm, D_MODEL), row),
            _full(ng.shape), _full(wg.shape),
            pl.BlockSpec((tm, SSM_WIDTH), row),
            pl.BlockSpec((tm, MLSTM_WIDTH), row),
            pl.BlockSpec((tm, ATTN_WIDTH), row),
            _full(wbr.shape), _full(wo.shape), _full(fg.shape),
        ],
        out_specs=pl.BlockSpec((tm, D_MODEL), row),
        out_shape=jax.ShapeDtypeStruct((t, D_MODEL), F32),
        compiler_params=_params("arbitrary"),
        name="merge",
    )(x2, ng, wg, ya, yb, yc, wbr, wo, fg)


def _row(v):
    return v.astype(F32).reshape(1, -1)


def _layer(x2, batch, seq, final_norm, norm_g, w_in, ssm_a_re, ssm_a_im, ssm_b_re, ssm_b_im,
           ssm_c_re, ssm_c_im, ssm_log_dt, ssm_d, glu_w, glu_b, qk_conv_w, qk_conv_b, igate_b,
           fgate_b, mh_norm_g, q_norm_g, w_uq, w_qidx, kidx_norm_g, kidx_norm_b, w_branch, w_out,
           final_norm_g):
    ng = _row(norm_g)
    o_a = 0
    o_b = o_a + 2 * SSM_WIDTH
    o_if = o_b + 3 * MLSTM_WIDTH
    o_zb = o_if + 2 * MLSTM_HEADS
    o_c = o_zb + MLSTM_WIDTH
    o_kidx = o_c + Q_LORA_RANK + 2 * ATTN_HEAD_DIM
    o_zc = o_kidx + IDX_HEAD_DIM + IDX_HEADS
    o_g = o_zc + ATTN_WIDTH

    wa = w_in[:, o_a:o_b].astype(BF16)
    wbu, cst, wc5 = _s5_tables(ssm_a_re, ssm_a_im, ssm_b_re, ssm_b_im, ssm_c_re, ssm_c_im, ssm_log_dt)
    ya = _s5_call(x2, ng, wa, wbu, cst, wc5, _row(ssm_d), glu_w.astype(BF16), _row(glu_b), batch, seq)

    wb = jnp.concatenate([w_in[:, o_b:o_if], w_in[:, o_zb:o_c]], axis=1).astype(BF16)
    wgt = jnp.pad(w_in[:, o_if:o_zb].astype(F32), ((0, 0), (0, LANES - 2 * MLSTM_HEADS)))
    gbias = jnp.concatenate([igate_b.astype(F32), fgate_b.astype(F32)])
    gbc = jnp.pad(gbias, (0, LANES - 2 * MLSTM_HEADS)).reshape(1, LANES)
    gbr = gbias.reshape(2 * MLSTM_HEADS, 1)
    yb = _mlstm_call(x2, ng, wb, wgt, qk_conv_w.astype(F32), _row(qk_conv_b), gbc, gbr,
                     _row(mh_norm_g), batch, seq)

    tail = jnp.pad(w_in[:, o_kidx:o_zc], ((0, 0), (0, LANES - IDX_HEAD_DIM - IDX_HEADS)))
    wcd = jnp.concatenate([w_in[:, o_c:o_kidx], w_in[:, o_zc:o_g], tail], axis=1).astype(BF16)
    qh, qit, wt, k, v, ki, zc = _dsaprep_call(
        x2, ng, wcd, _row(q_norm_g), w_uq.astype(BF16), w_qidx.astype(BF16),
        _row(kidx_norm_g), _row(kidx_norm_b))
    yc = _dsa_call(qh, qit, wt, zc, ki, k, v, batch, seq)

    wgm = w_in[:, o_g:].astype(BF16)
    return _merge_call(x2, ng, wgm, ya, yb, yc, w_branch.astype(BF16), w_out.astype(BF16),
                       _row(final_norm_g), final_norm)


def kernel(x, norm_g, w_in, ssm_a_re, ssm_a_im, ssm_b_re, ssm_b_im, ssm_c_re, ssm_c_im, ssm_log_dt, ssm_d, glu_w, glu_b, qk_conv_w, qk_conv_b, igate_b, fgate_b, mh_norm_g, q_norm_g, w_uq, w_qidx, kidx_norm_g, kidx_norm_b, w_branch, w_out, final_norm_g):
    batch, seq, d = x.shape
    depth = norm_g.shape[0]
    x2 = x.astype(F32).reshape(batch * seq, d)
    stacked = (norm_g, w_in, ssm_a_re, ssm_a_im, ssm_b_re, ssm_b_im, ssm_c_re, ssm_c_im, ssm_log_dt,
               ssm_d, glu_w, glu_b, qk_conv_w, qk_conv_b, igate_b, fgate_b, mh_norm_g, q_norm_g, w_uq,
               w_qidx, kidx_norm_g, kidx_norm_b, w_branch, w_out)
    for l in range(depth):
        x2 = _layer(x2, batch, seq, l == depth - 1, *[p[l] for p in stacked], final_norm_g)
    return x2.reshape(batch, seq, d).astype(x.dtype)
```

```python
import functools
import math

import jax
import jax.numpy as jnp
from jax import lax
from jax.experimental import pallas as pl
from jax.experimental.pallas import tpu as pltpu

F32 = jnp.float32
BF16 = jnp.bfloat16

D_MODEL = 1024
EPS = 1e-6
SSM_WIDTH = 512
SSM_GROUP = 16
SSM_GROUPS = 32
SSM_STATE = 64
SSM_LANES = SSM_GROUPS * SSM_STATE
SSM_BLOCKS = 4
MLSTM_HEADS = 4
MLSTM_HEAD_DIM = 128
MLSTM_WIDTH = 512
CONV_WIDTH = 4
ATTN_HEADS = 4
ATTN_HEAD_DIM = 128
ATTN_WIDTH = 512
Q_LORA_RANK = 256
IDX_HEADS = 8
IDX_HEAD_DIM = 64
TOPK_MAX = 256
Q_BLOCK = 128
N_BRANCH = 3

SUBLANES = 8
LANES = 128
S5_TILE = 256
MLSTM_TILE = 256
PREP_TILE = 256
MERGE_TILE = 256
KEY_TILE = 512
NEG_BIG = -1e30
LOG2E = math.log2(math.e)
VMEM_LIMIT = 56 * 1024 * 1024
MAX_BISECT = 48
COARSE_BISECT = 12
PACKED_ROWS = 16
STALL_ITERS = 3.0


def _dot(a, b):
    return jnp.dot(a, b, preferred_element_type=F32)


def _dot_nt(a, b):
    return lax.dot_general(a, b, (((1,), (1,)), ((), ())), preferred_element_type=F32)


def _dot_f32(a, b):
    return jnp.dot(a, b, preferred_element_type=F32, precision=lax.Precision.HIGHEST)


def _tree(fn, xs):
    xs = list(xs)
    while len(xs) > 1:
        xs = [fn(xs[i], xs[i + 1]) if i + 1 < len(xs) else xs[i] for i in range(0, len(xs), 2)]
    return xs[0]


def _floor_bf16(x):
    bits = pltpu.bitcast(x, jnp.int32)
    top = bits & jnp.int32(-65536)
    down = jnp.where(jnp.logical_and(bits < 0, (bits & jnp.int32(65535)) != 0), top + jnp.int32(65536), top)
    return pltpu.bitcast(down, F32)


def _rmsnorm(x, g):
    return x * lax.rsqrt(jnp.mean(x * x, axis=-1, keepdims=True) + EPS) * g


def _silu(x):
    return x * jax.nn.sigmoid(x)


def _params(*sem):
    return pltpu.CompilerParams(dimension_semantics=sem, vmem_limit_bytes=VMEM_LIMIT)


def _full(shape):
    n = len(shape)
    return pl.BlockSpec(shape, lambda *_: (0,) * n)


def _s5_kernel(x_ref, ng_ref, wa_ref, wbu_ref, cst_ref, wc_ref, d_ref, gw_ref, gb_ref,
               out_ref, bre, bim, car):
    tl = x_ref.shape[0]

    @pl.when(pl.program_id(1) == 0)
    def _():
        car[...] = jnp.zeros_like(car)

    h = _rmsnorm(x_ref[...], ng_ref[...]).astype(BF16)
    pa = _dot(h, wa_ref[...])
    u = pa[:, :SSM_WIDTH]
    z = pa[:, SSM_WIDTH:]
    ub = u.astype(BF16)
    half = SSM_LANES // SSM_BLOCKS
    for k in range(SSM_BLOCKS):
        blk = _dot(ub[:, k * LANES:(k + 1) * LANES], wbu_ref[k])
        bre[:, k * half:(k + 1) * half] = blk[:, :half]
        bim[:, k * half:(k + 1) * half] = blk[:, half:]

    def body(r, c):
        cr, ci = c
        rows = pl.ds(pl.multiple_of(r * SUBLANES, SUBLANES), SUBLANES)
        xr = bre[rows, :]
        xi = bim[rows, :]
        for lvl in range(3):
            ar = cst_ref[2 * lvl]
            ai = cst_ref[2 * lvl + 1]
            sr = pltpu.roll(xr, 1 << lvl, 0)
            si = pltpu.roll(xi, 1 << lvl, 0)
            xr, xi = xr + ar * sr - ai * si, xi + ar * si + ai * sr
        pr = cst_ref[6]
        pi = cst_ref[7]
        hr = xr + pr * cr - pi * ci
        hi = xi + pr * ci + pi * cr
        bre[rows, :] = hr
        bim[rows, :] = hi
        last = SUBLANES - 1
        return (jnp.broadcast_to(hr[last:, :], hr.shape), jnp.broadcast_to(hi[last:, :], hi.shape))

    cr, ci = lax.fori_loop(0, tl // SUBLANES, body, (car[0], car[1]))
    car[0] = cr
    car[1] = ci

    hre = bre[...].astype(BF16)
    him = bim[...].astype(BF16)
    ys = []
    for k in range(SSM_BLOCKS):
        hk = jnp.concatenate([hre[:, k * half:(k + 1) * half], him[:, k * half:(k + 1) * half]], axis=1)
        ys.append(_dot(hk, wc_ref[k]))
    y = jnp.concatenate(ys, axis=1) + d_ref[...] * u
    g = jax.nn.gelu(y)
    out = g * jax.nn.sigmoid(_dot(g.astype(BF16), gw_ref[...]) + gb_ref[...])
    out_ref[...] = (out * _silu(z)).astype(out_ref.dtype)


def _s5_tables(a_re, a_im, b_re, b_im, c_re, c_im, log_dt):
    dt = jnp.exp(log_dt.astype(F32))[:, None]
    ar = a_re.astype(F32)
    ai = a_im.astype(F32)

    def apow(k):
        mag = jnp.exp(k * ar * dt)
        ang = k * ai * dt
        return (mag * jnp.cos(ang)).reshape(-1), (mag * jnp.sin(ang)).reshape(-1)

    abr, abi = (mag.reshape(ar.shape) for mag in apow(1))
    den = ar * ar + ai * ai
    fr = ((abr - 1.0) * ar + abi * ai) / den
    fi = (abi * ar - (abr - 1.0) * ai) / den
    br = b_re.astype(F32)
    bi = b_im.astype(F32)
    bb_re = fr[..., None] * br - fi[..., None] * bi
    bb_im = fr[..., None] * bi + fi[..., None] * br
    gpb = SSM_GROUPS // SSM_BLOCKS
    eye = jnp.eye(gpb, dtype=F32)

    def bu_block(bb):
        bb4 = bb.reshape(SSM_BLOCKS, gpb, SSM_STATE, SSM_GROUP)
        w = jnp.einsum('kgnp,gh->kgphn', bb4, eye)
        return w.reshape(SSM_BLOCKS, gpb * SSM_GROUP, gpb * SSM_STATE)

    wbu = jnp.concatenate([bu_block(bb_re), bu_block(bb_im)], axis=2).astype(BF16)

    def c_block(c):
        c4 = c.astype(F32).reshape(SSM_BLOCKS, gpb, SSM_GROUP, SSM_STATE)
        w = jnp.einsum('kgpn,gh->kgnhp', c4, eye)
        return w.reshape(SSM_BLOCKS, gpb * SSM_STATE, gpb * SSM_GROUP)

    wc = jnp.concatenate([c_block(c_re), -c_block(c_im)], axis=1).astype(BF16)

    row = jnp.arange(SUBLANES, dtype=F32)[:, None]
    tabs = []
    for lvl in range(3):
        k = 1 << lvl
        pr, pi = apow(float(k))
        keep = row >= k
        tabs += [jnp.where(keep, pr[None, :], 0.0), jnp.where(keep, pi[None, :], 0.0)]
    prs, pis = zip(*[apow(float(r + 1)) for r in range(SUBLANES)])
    tabs += [jnp.stack(prs), jnp.stack(pis)]
    cst = jnp.stack(tabs).astype(F32)
    return wbu, cst, wc


def _s5_call(x2, ng, wa, wbu, cst, wc, dskip, gw, gb, batch, seq):
    tl = S5_TILE
    nt = seq // tl
    return pl.pallas_call(
        _s5_kernel,
        grid=(batch, nt),
        in_specs=[
            pl.BlockSpec((tl, D_MODEL), lambda b, i: (b * nt + i, 0)),
            _full(ng.shape), _full(wa.shape), _full(wbu.shape), _full(cst.shape), _full(wc.shape),
            _full(dskip.shape), _full(gw.shape), _full(gb.shape),
        ],
        out_specs=pl.BlockSpec((tl, SSM_WIDTH), lambda b, i: (b * nt + i, 0)),
        out_shape=jax.ShapeDtypeStruct((batch * seq, SSM_WIDTH), BF16),
        scratch_shapes=[
            pltpu.VMEM((tl, SSM_LANES), F32),
            pltpu.VMEM((tl, SSM_LANES), F32),
            pltpu.VMEM((2, SUBLANES, SSM_LANES), F32),
        ],
        compiler_params=_params("arbitrary", "arbitrary"),
        name="s5",
    )(x2, ng, wa, wbu, cst, wc, dskip, gw, gb)


def _mlstm_kernel(x_ref, ng_ref, wb_ref, wg_ref, cw_ref, cb_ref, gbc_ref, gbr_ref, mhg_ref,
                  out_ref, qk_scr, st_scr, m_scr):
    ct = x_ref.shape[0]
    hd = MLSTM_HEAD_DIM

    @pl.when(pl.program_id(1) == 0)
    def _():
        qk_scr[...] = jnp.zeros_like(qk_scr)
        st_scr[...] = jnp.zeros_like(st_scr)
        m_scr[...] = jnp.zeros_like(m_scr)

    h32 = _rmsnorm(x_ref[...], ng_ref[...])
    hb = h32.astype(BF16)
    pb = _dot(hb, wb_ref[...])
    gates = _dot_f32(h32, wg_ref[...])

    qk_scr[0:SUBLANES, :] = qk_scr[ct:ct + SUBLANES, :]
    qk_scr[SUBLANES:, :] = pb[:, :2 * MLSTM_WIDTH]
    conv = cb_ref[...]
    for j in range(CONV_WIDTH):
        off = SUBLANES - (CONV_WIDTH - 1) + j
        conv = conv + cw_ref[j:j + 1, :] * qk_scr[off:off + ct, :]
    qk = _silu(conv)
    q = qk[:, :MLSTM_WIDTH].astype(BF16)
    k = (qk[:, MLSTM_WIDTH:] * (hd ** -0.5)).astype(BF16)
    v = pb[:, 2 * MLSTM_WIDTH:3 * MLSTM_WIDTH].astype(BF16)
    z = pb[:, 3 * MLSTM_WIDTH:]

    gc = gates + gbc_ref[...]
    gr = gates.T[:SUBLANES, :] + gbr_ref[...]

    def logsig(t):
        return jnp.minimum(t, 0.0) - jnp.log1p(jnp.exp(-jnp.abs(t)))

    ti = lax.broadcasted_iota(jnp.int32, (ct, ct), 0)
    si = lax.broadcasted_iota(jnp.int32, (ct, ct), 1)
    causal = si <= ti
    lower = causal.astype(F32)
    upper = (ti <= si).astype(F32)
    bcum_c = _dot_f32(lower, logsig(gc))
    bcum_r = _dot_f32(logsig(gr), upper)
    ones_v = jnp.ones((ct, hd), BF16)

    outs = []
    for hh in range(MLSTM_HEADS):
        lo = hh * hd
        a_c = bcum_c[:, MLSTM_HEADS + hh:MLSTM_HEADS + hh + 1]
        i_c = gc[:, hh:hh + 1]
        a_r = bcum_r[MLSTM_HEADS + hh:MLSTM_HEADS + hh + 1, :]
        i_r = gr[hh:hh + 1, :]
        m_prev = m_scr[hh:hh + 1, 0:1]
        b_last = a_c[ct - 1:ct, :]

        dmat = jnp.where(causal, a_c - a_r + i_r, -jnp.inf)
        inter = a_c + m_prev
        m = jnp.maximum(inter, jnp.max(dmat, axis=-1, keepdims=True))
        dexp = jnp.exp(dmat - m)
        inter_w = jnp.exp(inter - m)

        qh = q[:, lo:lo + hd]
        kh = k[:, lo:lo + hd]
        v_aug = jnp.concatenate([v[:, lo:lo + hd], ones_v], axis=1)
        s = (_dot_nt(qh, kh) * dexp).astype(BF16)
        st = st_scr[hh]
        numden = _dot(s, v_aug) + inter_w * _dot(qh, st.astype(BF16))
        num = numden[:, :hd]
        den = numden[:, hd:hd + 1]
        hout = num / jnp.maximum(jnp.abs(den), jnp.exp(-m))

        g_end = b_last - a_c + i_c
        m_next = jnp.maximum(b_last + m_prev, jnp.max(b_last - a_r + i_r, axis=-1, keepdims=True))
        w_s = jnp.exp(g_end - m_next)
        decay = jnp.exp(b_last + m_prev - m_next)
        kw = (kh.astype(F32) * w_s).T.astype(BF16)
        st_scr[hh] = decay * st + _dot(kw, v_aug)
        m_scr[hh:hh + 1, :] = jnp.broadcast_to(m_next, (1, LANES))

        hn = hout * lax.rsqrt(jnp.mean(hout * hout, axis=-1, keepdims=True) + EPS)
        outs.append(hn)
    hcat = jnp.concatenate(outs, axis=1) * mhg_ref[...]
    out_ref[...] = (hcat * _silu(z)).astype(out_ref.dtype)


def _mlstm_call(x2, ng, wb, wg, cw, cb, gbc, gbr, mhg, batch, seq):
    ct = MLSTM_TILE
    nt = seq // ct
    return pl.pallas_call(
        _mlstm_kernel,
        grid=(batch, nt),
        in_specs=[
            pl.BlockSpec((ct, D_MODEL), lambda b, i: (b * nt + i, 0)),
            _full(ng.shape), _full(wb.shape), _full(wg.shape), _full(cw.shape), _full(cb.shape),
            _full(gbc.shape), _full(gbr.shape), _full(mhg.shape),
        ],
        out_specs=pl.BlockSpec((ct, MLSTM_WIDTH), lambda b, i: (b * nt + i, 0)),
        out_shape=jax.ShapeDtypeStruct((batch * seq, MLSTM_WIDTH), BF16),
        scratch_shapes=[
            pltpu.VMEM((ct + SUBLANES, 2 * MLSTM_WIDTH), F32),
            pltpu.VMEM((MLSTM_HEADS, MLSTM_HEAD_DIM, 2 * MLSTM_HEAD_DIM), F32),
            pltpu.VMEM((SUBLANES, LANES), F32),
        ],
        compiler_params=_params("arbitrary", "arbitrary"),
        name="mlstm",
    )(x2, ng, wb, wg, cw, cb, gbc, gbr, mhg)


def _dsaprep_kernel(x_ref, ng_ref, wc_ref, qg_ref, wuq_ref, wqi_ref, kg_ref, kb_ref,
                    qh_ref, qit_ref, wt_ref, k_ref, v_ref, ki_ref, zc_ref):
    tm = x_ref.shape[0]
    hb = _rmsnorm(x_ref[...], ng_ref[...]).astype(BF16)
    pc = _dot(hb, wc_ref[...])
    cq = pc[:, :Q_LORA_RANK]
    o = Q_LORA_RANK
    k_ref[...] = pc[:, o:o + ATTN_HEAD_DIM].astype(BF16)
    v_ref[...] = pc[:, o + ATTN_HEAD_DIM:o + 2 * ATTN_HEAD_DIM].astype(BF16)
    o += 2 * ATTN_HEAD_DIM
    zc_ref[...] = _silu(pc[:, o:o + ATTN_WIDTH]).astype(BF16)
    o += ATTN_WIDTH
    tail = pc[:, o:o + LANES]
    kidx = tail[:, :IDX_HEAD_DIM]
    mu = jnp.mean(kidx, axis=-1, keepdims=True)
    kc = kidx - mu
    ki = kc * lax.rsqrt(jnp.mean(kc * kc, axis=-1, keepdims=True) + EPS) * kg_ref[...] + kb_ref[...]
    ki_ref[...] = ki.astype(BF16)
    wi = tail * ((IDX_HEADS ** -0.5) * (IDX_HEAD_DIM ** -0.5))

    cqn = _rmsnorm(cq, qg_ref[...]).astype(BF16)
    q = (_dot(cqn, wuq_ref[...]) * (ATTN_HEAD_DIM ** -0.5 * LOG2E)).astype(BF16)
    qi = _dot(cqn, wqi_ref[...])
    for b in range(tm // Q_BLOCK):
        rows = slice(b * Q_BLOCK, (b + 1) * Q_BLOCK)
        for hh in range(ATTN_HEADS):
            qh_ref[b, hh * Q_BLOCK:(hh + 1) * Q_BLOCK, :] = q[rows, hh * ATTN_HEAD_DIM:(hh + 1) * ATTN_HEAD_DIM]
        qit = qi[rows, :].T
        qit_ref[b] = jnp.concatenate(
            [qit[hh * IDX_HEAD_DIM:(hh + 1) * IDX_HEAD_DIM, :] for hh in range(IDX_HEADS)], axis=1).astype(BF16)
        wt_ref[b] = wi[rows, :].T[IDX_HEAD_DIM:IDX_HEAD_DIM + IDX_HEADS, :]


def _dsaprep_call(x2, ng, wc, qg, wuq, wqi, kg, kb):
    t = x2.shape[0]
    tm = PREP_TILE
    nq = tm // Q_BLOCK
    row = lambda i: (i, 0)
    blk = lambda i: (i, 0, 0)
    return pl.pallas_call(
        _dsaprep_kernel,
        grid=(t // tm,),
        in_specs=[
            pl.BlockSpec((tm, D_MODEL), row),
            _full(ng.shape), _full(wc.shape), _full(qg.shape), _full(wuq.shape), _full(wqi.shape),
            _full(kg.shape), _full(kb.shape),
        ],
        out_specs=[
            pl.BlockSpec((nq, ATTN_HEADS * Q_BLOCK, ATTN_HEAD_DIM), blk),
            pl.BlockSpec((nq, IDX_HEAD_DIM, IDX_HEADS * Q_BLOCK), blk),
            pl.BlockSpec((nq, IDX_HEADS, Q_BLOCK), blk),
            pl.BlockSpec((tm, ATTN_HEAD_DIM), row),
            pl.BlockSpec((tm, ATTN_HEAD_DIM), row),
            pl.BlockSpec((tm, IDX_HEAD_DIM), row),
            pl.BlockSpec((tm, ATTN_WIDTH), row),
        ],
        out_shape=[
            jax.ShapeDtypeStruct((t // Q_BLOCK, ATTN_HEADS * Q_BLOCK, ATTN_HEAD_DIM), BF16),
            jax.ShapeDtypeStruct((t // Q_BLOCK, IDX_HEAD_DIM, IDX_HEADS * Q_BLOCK), BF16),
            jax.ShapeDtypeStruct((t // Q_BLOCK, IDX_HEADS, Q_BLOCK), F32),
            jax.ShapeDtypeStruct((t, ATTN_HEAD_DIM), BF16),
            jax.ShapeDtypeStruct((t, ATTN_HEAD_DIM), BF16),
            jax.ShapeDtypeStruct((t, IDX_HEAD_DIM), BF16),
            jax.ShapeDtypeStruct((t, ATTN_WIDTH), BF16),
        ],
        compiler_params=_params("arbitrary"),
        name="dsaprep",
    )(x2, ng, wc, qg, wuq, wqi, kg, kb)


def _dsa_kernel(qh_ref, qit_ref, wt_ref, zc_ref, ki_ref, k_ref, v_ref, out_ref,
                sc_scr, sc16_scr, sct_scr, lg_scr, m_scr, acc_scr, *, topk, seq):
    j = pl.program_id(1)
    qb = Q_BLOCK
    kt_rows = KEY_TILE
    groups = kt_rows // SUBLANES
    n_t = (j * qb + qb + kt_rows - 1) // kt_rows
    t_pos = j * qb + lax.broadcasted_iota(jnp.int32, (kt_rows, qb), 1)
    s_loc = lax.broadcasted_iota(jnp.int32, (kt_rows, qb), 0)
    kf = float(topk)

    def tile_rows(kt):
        if isinstance(kt, int):
            return pl.ds(kt * kt_rows, kt_rows)
        return pl.ds(pl.multiple_of(kt * kt_rows, kt_rows), kt_rows)

    def chunk_lanes(kt, c):
        if isinstance(kt, int):
            return pl.ds(kt * kt_rows + c * LANES, LANES)
        return pl.ds(pl.multiple_of(kt * kt_rows + c * LANES, LANES), LANES)

    def fold(fn, x):
        return _tree(fn, [x[g * SUBLANES:(g + 1) * SUBLANES, :] for g in range(groups)])

    qit = qit_ref[0]
    wt = wt_ref[0]

    def score_tile(kt, carry):
        mx, mn = carry
        rows = tile_rows(kt)
        z = _dot(ki_ref[rows, :], qit)
        sc = None
        for hh in range(IDX_HEADS):
            r = jnp.maximum(z[:, hh * qb:(hh + 1) * qb], 0.0) * wt[hh:hh + 1, :]
            sc = r if sc is None else sc + r
        causal = kt * kt_rows + s_loc <= t_pos
        masked = jnp.where(causal, sc, -jnp.inf)
        sc_scr[rows, :] = masked
        sc16_scr[rows, :] = _floor_bf16(masked).astype(BF16)
        sct_scr[:, tile_rows(kt)] = masked.T
        mx = jnp.maximum(mx, fold(jnp.maximum, masked))
        mn = jnp.minimum(mn, fold(jnp.minimum, jnp.where(causal, sc, jnp.inf)))
        lg_scr[:, tile_rows(kt)] = _dot_nt(qh_ref[0], k_ref[rows, :])
        return mx, mn

    mx8, mn8 = lax.fori_loop(0, n_t, score_tile,
                             (jnp.full((SUBLANES, qb), -jnp.inf, F32), jnp.full((SUBLANES, qb), jnp.inf, F32)))
    row_max = jnp.max(mx8, axis=0, keepdims=True)
    row_min = jnp.min(mn8, axis=0, keepdims=True)

    def count_ge(thr):
        thr8 = jnp.broadcast_to(thr, (SUBLANES, qb))

        def step(kt, acc):
            sc = sc_scr[tile_rows(kt), :]
            hit = [jnp.where(sc[g * SUBLANES:(g + 1) * SUBLANES, :] >= thr8, 1.0, 0.0) for g in range(groups)]
            return acc + _tree(jnp.add, hit)

        acc = lax.fori_loop(0, n_t, step, jnp.zeros((SUBLANES, qb), F32))
        return jnp.sum(acc, axis=0, keepdims=True)

    def count_ge16(thr):
        thr16 = jnp.broadcast_to(thr, (PACKED_ROWS, qb)).astype(BF16)
        one = jnp.ones((PACKED_ROWS, qb), BF16)
        nil = jnp.zeros((PACKED_ROWS, qb), BF16)

        def step(kt, acc):
            sc = sc16_scr[tile_rows(kt), :]
            hit = [jnp.where(sc[g * PACKED_ROWS:(g + 1) * PACKED_ROWS, :] >= thr16, one, nil)
                   for g in range(kt_rows // PACKED_ROWS)]
            return acc + _tree(jnp.add, hit).astype(F32)

        acc = lax.fori_loop(0, n_t, step, jnp.zeros((PACKED_ROWS, qb), F32))
        return jnp.sum(acc, axis=0, keepdims=True)

    def max_below(bound):
        def step(kt, acc):
            sc = sc_scr[tile_rows(kt), :]
            return jnp.maximum(acc, fold(jnp.maximum, jnp.where(sc < bound, sc, -jnp.inf)))

        acc = lax.fori_loop(0, n_t, step, jnp.full((SUBLANES, qb), -jnp.inf, F32))
        return jnp.max(acc, axis=0, keepdims=True)

    def flag(cond):
        return jnp.where(cond, 1.0, 0.0)

    n_causal = (t_pos[0:1, :] + 1).astype(F32)
    done0 = flag(n_causal <= kf)
    zero = jnp.zeros((1, qb), F32)

    def bis_cond(c):
        it, _, _, _, _, done, stall, _ = c
        return jnp.logical_and(it < MAX_BISECT, jnp.min(jnp.maximum(done, stall)) < 0.5)

    def bis_body(c, count=count_ge, snap=lambda x: x, may_stall=True):
        it, lo, hi, clo, chi, done, stall, same = c
        mid = snap(0.5 * lo + 0.5 * hi)
        inside = jnp.logical_and(mid > lo, mid < hi)
        cnt = count(mid)
        act = jnp.logical_and(jnp.maximum(done, stall) < 0.5, inside)
        up = jnp.logical_and(act, cnt >= kf)
        dn = jnp.logical_and(act, cnt < kf)
        nlo = jnp.where(up, mid, lo)
        nclo = jnp.where(up, cnt, clo)
        nhi = jnp.where(dn, mid, hi)
        nchi = jnp.where(dn, cnt, chi)
        done = jnp.maximum(done, flag(nclo == kf))
        if may_stall:
            same = jnp.where((nclo - nchi) == (clo - chi), same + 1.0, 0.0)
            stall = jnp.maximum(stall, flag(jnp.logical_or(jnp.logical_not(inside), same >= STALL_ITERS)))
        return it + 1, nlo, nhi, nclo, nchi, done, stall, same

    snap = lambda x: x.astype(BF16).astype(F32)
    lo0 = _floor_bf16(row_min)
    hi0 = snap(row_max + (jnp.abs(row_max) + 1.0) * 2.0 ** -6)
    state = (jnp.int32(0), lo0, hi0, n_causal, zero, done0, zero, zero)
    state = lax.fori_loop(0, COARSE_BISECT,
                          lambda _, c: bis_body(c, count=count_ge16, snap=snap, may_stall=False), state)
    _, lo, hi, clo, chi, done, _, _ = lax.while_loop(bis_cond, lambda c: bis_body(bis_body(c)), state)

    def walk_cond(c):
        it, _, _, _, _, done, _ = c
        return jnp.logical_and(it < seq, jnp.min(done) < 0.5)

    def walk_body(c):
        it, lo, hi, clo, chi, done, tie = c
        cand = max_below(hi)
        cge = count_ge(cand)
        act = done < 0.5
        fin = jnp.logical_and(act, cge >= kf)
        mv = jnp.logical_and(act, cge < kf)
        lo = jnp.where(fin, cand, lo)
        clo = jnp.where(fin, cge, clo)
        tie = jnp.maximum(tie, flag(jnp.logical_and(fin, cge > kf)))
        hi = jnp.where(mv, cand, hi)
        chi = jnp.where(mv, cge, chi)
        done = jnp.maximum(done, flag(fin))
        return it + 1, lo, hi, clo, chi, done, tie

    _, thr, _, _, chi, _, tie = lax.while_loop(
        walk_cond, walk_body, (jnp.int32(0), lo, hi, clo, chi, done, zero))
    thr8 = jnp.broadcast_to(thr, (SUBLANES, qb))

    @pl.when(jnp.max(tie) > 0.5)
    def _():
        need = jnp.where(tie > 0.5, kf - chi, float(seq))
        lower = (lax.broadcasted_iota(jnp.int32, (kt_rows, kt_rows), 1)
                 <= lax.broadcasted_iota(jnp.int32, (kt_rows, kt_rows), 0))
        lower = jnp.where(lower, 1.0, 0.0).astype(BF16)

        def step(kt, seen):
            rows = tile_rows(kt)
            sc = sc_scr[rows, :]
            eq = sc == thr
            rank = _dot(lower, jnp.where(eq, 1.0, 0.0).astype(BF16)) + seen
            kept = jnp.where(jnp.logical_and(eq, rank > need), -jnp.inf, sc)
            sc_scr[rows, :] = kept
            sct_scr[:, tile_rows(kt)] = kept.T
            return rank[kt_rows - 1:kt_rows, :]

        lax.fori_loop(0, n_t, step, zero)

    chunks = kt_rows // LANES
    thr_col = jnp.broadcast_to(thr, (qb, qb)).T

    m_scr[...] = jnp.full_like(m_scr, NEG_BIG)

    def max_tile(kt, carry):
        bias = []
        for c in range(chunks):
            lanes = chunk_lanes(kt, c)
            b = jnp.where(sct_scr[:, lanes] >= thr_col, 0.0, NEG_BIG)
            sct_scr[:, lanes] = b
            bias.append(b)
        for hh in range(ATTN_HEADS):
            hrows = slice(hh * qb, (hh + 1) * qb)
            m = m_scr[hrows, :]
            for c in range(chunks):
                m = jnp.maximum(m, lg_scr[hrows, chunk_lanes(kt, c)] + bias[c])
            m_scr[hrows, :] = m
        return carry

    lax.fori_loop(0, n_t, max_tile, 0)
    m_scr[...] = jnp.broadcast_to(jnp.max(m_scr[...], axis=1, keepdims=True), m_scr.shape)

    acc_scr[...] = jnp.zeros_like(acc_scr)
    ones_blk = jnp.ones((kt_rows, ATTN_HEAD_DIM), BF16)

    def attn_tile(kt, carry):
        v_aug = jnp.concatenate([v_ref[tile_rows(kt), :], ones_blk], axis=1)
        for hh in range(ATTN_HEADS):
            hrows = slice(hh * qb, (hh + 1) * qb)
            m = m_scr[hrows, :]
            p = [jnp.exp2(lg_scr[hrows, chunk_lanes(kt, c)] + sct_scr[:, chunk_lanes(kt, c)] - m).astype(BF16)
                 for c in range(chunks)]
            acc_scr[hrows, :] += _dot(jnp.concatenate(p, axis=1), v_aug)
        return carry

    lax.fori_loop(0, n_t, attn_tile, 0)
    acc = acc_scr[...]
    o = jnp.concatenate(
        [acc[hh * qb:(hh + 1) * qb, :ATTN_HEAD_DIM] / acc[hh * qb:(hh + 1) * qb, ATTN_HEAD_DIM:]
         for hh in range(ATTN_HEADS)], axis=1)
    out_ref[...] = (o * zc_ref[...].astype(F32)).astype(out_ref.dtype)


def _dsa_call(qh, qit, wt, zc, ki, k, v, batch, seq):
    nb = seq // Q_BLOCK
    topk = min(TOPK_MAX, seq // 4)
    blk3 = lambda b, j: (b * nb + j, 0, 0)
    blk2 = lambda b, j: (b * nb + j, 0)
    per_b = lambda b, j: (b, 0)
    return pl.pallas_call(
        functools.partial(_dsa_kernel, topk=topk, seq=seq),
        grid=(batch, nb),
        in_specs=[
            pl.BlockSpec((1, ATTN_HEADS * Q_BLOCK, ATTN_HEAD_DIM), blk3),
            pl.BlockSpec((1, IDX_HEAD_DIM, IDX_HEADS * Q_BLOCK), blk3),
            pl.BlockSpec((1, IDX_HEADS, Q_BLOCK), blk3),
            pl.BlockSpec((Q_BLOCK, ATTN_WIDTH), blk2),
            pl.BlockSpec((seq, IDX_HEAD_DIM), per_b),
            pl.BlockSpec((seq, ATTN_HEAD_DIM), per_b),
            pl.BlockSpec((seq, ATTN_HEAD_DIM), per_b),
        ],
        out_specs=pl.BlockSpec((Q_BLOCK, ATTN_WIDTH), blk2),
        out_shape=jax.ShapeDtypeStruct((batch * seq, ATTN_WIDTH), BF16),
        scratch_shapes=[
            pltpu.VMEM((seq, Q_BLOCK), F32),
            pltpu.VMEM((seq, Q_BLOCK), BF16),
            pltpu.VMEM((Q_BLOCK, seq + LANES), F32),
            pltpu.VMEM((ATTN_HEADS * Q_BLOCK, seq + LANES), F32),
            pltpu.VMEM((ATTN_HEADS * Q_BLOCK, LANES), F32),
            pltpu.VMEM((ATTN_HEADS * Q_BLOCK, 2 * ATTN_HEAD_DIM), F32),
        ],
        compiler_params=_params("arbitrary", "arbitrary"),
        name="dsa",
    )(qh, qit, wt, zc, ki, k, v)


def _merge_kernel(x_ref, ng_ref, wg_ref, ya_ref, yb_ref, yc_ref, wbr_ref, wo_ref, fg_ref, out_ref,
                  *, final_norm):
    x = x_ref[...]
    hb = _rmsnorm(x, ng_ref[...]).astype(BF16)
    merged = None
    for n, y_ref in enumerate((ya_ref, yb_ref, yc_ref)):
        gate = jax.nn.sigmoid(_dot(hb, wg_ref[:, n * D_MODEL:(n + 1) * D_MODEL]))
        term = gate * _dot(y_ref[...], wbr_ref[n])
        merged = term if merged is None else merged + term
    y = x + _dot(merged.astype(BF16), wo_ref[...])
    if final_norm:
        y = _rmsnorm(y, fg_ref[...])
    out_ref[...] = y


def _merge_call(x2, ng, wg, ya, yb, yc, wbr, wo, fg, final_norm):
    t = x2.shape[0]
    tm = MERGE_TILE
    row = lambda i: (i, 0)
    return pl.pallas_call(
        functools.partial(_merge_kernel, final_norm=final_norm),
        grid=(t // tm,),
        in_specs=[
            pl.BlockSpec((tm, D_MODEL), row),
            _full(ng.shape), _full(wg.shape),
            pl.BlockSpec((tm, SSM_WIDTH), row),
            pl.BlockSpec((tm, MLSTM_WIDTH), row),
            pl.BlockSpec((tm, ATTN_WIDTH), row),
            _full(wbr.shape), _full(wo.shape), _full(fg.shape),
        ],
        out_specs=pl.BlockSpec((tm, D_MODEL), row),
        out_shape=jax.ShapeDtypeStruct((t, D_MODEL), F32),
        compiler_params=_params("arbitrary"),
        name="merge",
    )(x2, ng, wg, ya, yb, yc, wbr, wo, fg)


def _row(v):
    return v.astype(F32).reshape(1, -1)


def _layer(x2, batch, seq, final_norm, norm_g, w_in, ssm_a_re, ssm_a_im, ssm_b_re, ssm_b_im,
           ssm_c_re, ssm_c_im, ssm_log_dt, ssm_d, glu_w, glu_b, qk_conv_w, qk_conv_b, igate_b,
           fgate_b, mh_norm_g, q_norm_g, w_uq, w_qidx, kidx_norm_g, kidx_norm_b, w_branch, w_out,
           final_norm_g):
    ng = _row(norm_g)
    o_a = 0
    o_b = o_a + 2 * SSM_WIDTH
    o_if = o_b + 3 * MLSTM_WIDTH
    o_zb = o_if + 2 * MLSTM_HEADS
    o_c = o_zb + MLSTM_WIDTH
    o_kidx = o_c + Q_LORA_RANK + 2 * ATTN_HEAD_DIM
    o_zc = o_kidx + IDX_HEAD_DIM + IDX_HEADS
    o_g = o_zc + ATTN_WIDTH

    wa = w_in[:, o_a:o_b].astype(BF16)
    wbu, cst, wc5 = _s5_tables(ssm_a_re, ssm_a_im, ssm_b_re, ssm_b_im, ssm_c_re, ssm_c_im, ssm_log_dt)
    ya = _s5_call(x2, ng, wa, wbu, cst, wc5, _row(ssm_d), glu_w.astype(BF16), _row(glu_b), batch, seq)

    wb = jnp.concatenate([w_in[:, o_b:o_if], w_in[:, o_zb:o_c]], axis=1).astype(BF16)
    wgt = jnp.pad(w_in[:, o_if:o_zb].astype(F32), ((0, 0), (0, LANES - 2 * MLSTM_HEADS)))
    gbias = jnp.concatenate([igate_b.astype(F32), fgate_b.astype(F32)])
    gbc = jnp.pad(gbias, (0, LANES - 2 * MLSTM_HEADS)).reshape(1, LANES)
    gbr = gbias.reshape(2 * MLSTM_HEADS, 1)
    yb = _mlstm_call(x2, ng, wb, wgt, qk_conv_w.astype(F32), _row(qk_conv_b), gbc, gbr,
                     _row(mh_norm_g), batch, seq)

    tail = jnp.pad(w_in[:, o_kidx:o_zc], ((0, 0), (0, LANES - IDX_HEAD_DIM - IDX_HEADS)))
    wcd = jnp.concatenate([w_in[:, o_c:o_kidx], w_in[:, o_zc:o_g], tail], axis=1).astype(BF16)
    qh, qit, wt, k, v, ki, zc = _dsaprep_call(
        x2, ng, wcd, _row(q_norm_g), w_uq.astype(BF16), w_qidx.astype(BF16),
        _row(kidx_norm_g), _row(kidx_norm_b))
    yc = _dsa_call(qh, qit, wt, zc, ki, k, v, batch, seq)

    wgm = w_in[:, o_g:].astype(BF16)
    return _merge_call(x2, ng, wgm, ya, yb, yc, w_branch.astype(BF16), w_out.astype(BF16),
                       _row(final_norm_g), final_norm)


def kernel(x, norm_g, w_in, ssm_a_re, ssm_a_im, ssm_b_re, ssm_b_im, ssm_c_re, ssm_c_im, ssm_log_dt, ssm_d, glu_w, glu_b, qk_conv_w, qk_conv_b, igate_b, fgate_b, mh_norm_g, q_norm_g, w_uq, w_qidx, kidx_norm_g, kidx_norm_b, w_branch, w_out, final_norm_g):
    batch, seq, d = x.shape
    depth = norm_g.shape[0]
    x2 = x.astype(F32).reshape(batch * seq, d)
    stacked = (norm_g, w_in, ssm_a_re, ssm_a_im, ssm_b_re, ssm_b_im, ssm_c_re, ssm_c_im, ssm_log_dt,
               ssm_d, glu_w, glu_b, qk_conv_w, qk_conv_b, igate_b, fgate_b, mh_norm_g, q_norm_g, w_uq,
               w_qidx, kidx_norm_g, kidx_norm_b, w_branch, w_out)
    for l in range(depth):
        x2 = _layer(x2, batch, seq, l == depth - 1, *[p[l] for p in stacked], final_norm_g)
    return x2.reshape(batch, seq, d).astype(x.dtype)
```

```python
import functools
import math

import jax
import jax.numpy as jnp
from jax import lax
from jax.experimental import pallas as pl
from jax.experimental.pallas import tpu as pltpu

F32 = jnp.float32
BF16 = jnp.bfloat16

D_MODEL = 1024
EPS = 1e-6
SSM_WIDTH = 512
SSM_GROUP = 16
SSM_GROUPS = 32
SSM_STATE = 64
SSM_LANES = SSM_GROUPS * SSM_STATE
SSM_BLOCKS = 4
MLSTM_HEADS = 4
MLSTM_HEAD_DIM = 128
MLSTM_WIDTH = 512
CONV_WIDTH = 4
ATTN_HEADS = 4
ATTN_HEAD_DIM = 128
ATTN_WIDTH = 512
Q_LORA_RANK = 256
IDX_HEADS = 8
IDX_HEAD_DIM = 64
TOPK_MAX = 256
Q_BLOCK = 128
N_BRANCH = 3

SUBLANES = 8
LANES = 128
S5_TILE = 256
MLSTM_TILE = 256
PREP_TILE = 256
MERGE_TILE = 256
KEY_TILE = 512
NEG_BIG = -1e30
LOG2E = math.log2(math.e)
VMEM_LIMIT = 56 * 1024 * 1024
MAX_BISECT = 48
BLIND_BISECT = 14
STALL_ITERS = 3.0


def _dot(a, b):
    return jnp.dot(a, b, preferred_element_type=F32)


def _dot_nt(a, b):
    return lax.dot_general(a, b, (((1,), (1,)), ((), ())), preferred_element_type=F32)


def _dot_f32(a, b):
    return jnp.dot(a, b, preferred_element_type=F32, precision=lax.Precision.HIGHEST)


def _tree(fn, xs):
    xs = list(xs)
    while len(xs) > 1:
        xs = [fn(xs[i], xs[i + 1]) if i + 1 < len(xs) else xs[i] for i in range(0, len(xs), 2)]
    return xs[0]


def _rmsnorm(x, g):
    return x * lax.rsqrt(jnp.mean(x * x, axis=-1, keepdims=True) + EPS) * g


def _silu(x):
    return x * jax.nn.sigmoid(x)


def _params(*sem):
    return pltpu.CompilerParams(dimension_semantics=sem, vmem_limit_bytes=VMEM_LIMIT)


def _full(shape):
    n = len(shape)
    return pl.BlockSpec(shape, lambda *_: (0,) * n)


def _s5_kernel(x_ref, ng_ref, wa_ref, wbu_ref, cst_ref, wc_ref, d_ref, gw_ref, gb_ref,
               out_ref, bre, bim, car):
    tl = x_ref.shape[0]

    @pl.when(pl.program_id(1) == 0)
    def _():
        car[...] = jnp.zeros_like(car)

    h = _rmsnorm(x_ref[...], ng_ref[...]).astype(BF16)
    pa = _dot(h, wa_ref[...])
    u = pa[:, :SSM_WIDTH]
    z = pa[:, SSM_WIDTH:]
    ub = u.astype(BF16)
    half = SSM_LANES // SSM_BLOCKS
    for k in range(SSM_BLOCKS):
        blk = _dot(ub[:, k * LANES:(k + 1) * LANES], wbu_ref[k])
        bre[:, k * half:(k + 1) * half] = blk[:, :half]
        bim[:, k * half:(k + 1) * half] = blk[:, half:]

    def body(r, c):
        cr, ci = c
        rows = pl.ds(pl.multiple_of(r * SUBLANES, SUBLANES), SUBLANES)
        xr = bre[rows, :]
        xi = bim[rows, :]
        for lvl in range(3):
            ar = cst_ref[2 * lvl]
            ai = cst_ref[2 * lvl + 1]
            sr = pltpu.roll(xr, 1 << lvl, 0)
            si = pltpu.roll(xi, 1 << lvl, 0)
            xr, xi = xr + ar * sr - ai * si, xi + ar * si + ai * sr
        pr = cst_ref[6]
        pi = cst_ref[7]
        hr = xr + pr * cr - pi * ci
        hi = xi + pr * ci + pi * cr
        bre[rows, :] = hr
        bim[rows, :] = hi
        last = SUBLANES - 1
        return (jnp.broadcast_to(hr[last:, :], hr.shape), jnp.broadcast_to(hi[last:, :], hi.shape))

    cr, ci = lax.fori_loop(0, tl // SUBLANES, body, (car[0], car[1]))
    car[0] = cr
    car[1] = ci

    hre = bre[...].astype(BF16)
    him = bim[...].astype(BF16)
    ys = []
    for k in range(SSM_BLOCKS):
        hk = jnp.concatenate([hre[:, k * half:(k + 1) * half], him[:, k * half:(k + 1) * half]], axis=1)
        ys.append(_dot(hk, wc_ref[k]))
    y = jnp.concatenate(ys, axis=1) + d_ref[...] * u
    g = jax.nn.gelu(y)
    out = g * jax.nn.sigmoid(_dot(g.astype(BF16), gw_ref[...]) + gb_ref[...])
    out_ref[...] = (out * _silu(z)).astype(out_ref.dtype)


def _s5_tables(a_re, a_im, b_re, b_im, c_re, c_im, log_dt):
    dt = jnp.exp(log_dt.astype(F32))[:, None]
    ar = a_re.astype(F32)
    ai = a_im.astype(F32)

    def apow(k):
        mag = jnp.exp(k * ar * dt)
        ang = k * ai * dt
        return (mag * jnp.cos(ang)).reshape(-1), (mag * jnp.sin(ang)).reshape(-1)

    abr, abi = (mag.reshape(ar.shape) for mag in apow(1))
    den = ar * ar + ai * ai
    fr = ((abr - 1.0) * ar + abi * ai) / den
    fi = (abi * ar - (abr - 1.0) * ai) / den
    br = b_re.astype(F32)
    bi = b_im.astype(F32)
    bb_re = fr[..., None] * br - fi[..., None] * bi
    bb_im = fr[..., None] * bi + fi[..., None] * br
    gpb = SSM_GROUPS // SSM_BLOCKS
    eye = jnp.eye(gpb, dtype=F32)

    def bu_block(bb):
        bb4 = bb.reshape(SSM_BLOCKS, gpb, SSM_STATE, SSM_GROUP)
        w = jnp.einsum('kgnp,gh->kgphn', bb4, eye)
        return w.reshape(SSM_BLOCKS, gpb * SSM_GROUP, gpb * SSM_STATE)

    wbu = jnp.concatenate([bu_block(bb_re), bu_block(bb_im)], axis=2).astype(BF16)

    def c_block(c):
        c4 = c.astype(F32).reshape(SSM_BLOCKS, gpb, SSM_GROUP, SSM_STATE)
        w = jnp.einsum('kgpn,gh->kgnhp', c4, eye)
        return w.reshape(SSM_BLOCKS, gpb * SSM_STATE, gpb * SSM_GROUP)

    wc = jnp.concatenate([c_block(c_re), -c_block(c_im)], axis=1).astype(BF16)

    row = jnp.arange(SUBLANES, dtype=F32)[:, None]
    tabs = []
    for lvl in range(3):
        k = 1 << lvl
        pr, pi = apow(float(k))
        keep = row >= k
        tabs += [jnp.where(keep, pr[None, :], 0.0), jnp.where(keep, pi[None, :], 0.0)]
    prs, pis = zip(*[apow(float(r + 1)) for r in range(SUBLANES)])
    tabs += [jnp.stack(prs), jnp.stack(pis)]
    cst = jnp.stack(tabs).astype(F32)
    return wbu, cst, wc


def _s5_call(x2, ng, wa, wbu, cst, wc, dskip, gw, gb, batch, seq):
    tl = S5_TILE
    nt = seq // tl
    return pl.pallas_call(
        _s5_kernel,
        grid=(batch, nt),
        in_specs=[
            pl.BlockSpec((tl, D_MODEL), lambda b, i: (b * nt + i, 0)),
            _full(ng.shape), _full(wa.shape), _full(wbu.shape), _full(cst.shape), _full(wc.shape),
            _full(dskip.shape), _full(gw.shape), _full(gb.shape),
        ],
        out_specs=pl.BlockSpec((tl, SSM_WIDTH), lambda b, i: (b * nt + i, 0)),
        out_shape=jax.ShapeDtypeStruct((batch * seq, SSM_WIDTH), BF16),
        scratch_shapes=[
            pltpu.VMEM((tl, SSM_LANES), F32),
            pltpu.VMEM((tl, SSM_LANES), F32),
            pltpu.VMEM((2, SUBLANES, SSM_LANES), F32),
        ],
        compiler_params=_params("arbitrary", "arbitrary"),
        name="s5",
    )(x2, ng, wa, wbu, cst, wc, dskip, gw, gb)


def _mlstm_kernel(x_ref, ng_ref, wb_ref, wg_ref, cw_ref, cb_ref, gbc_ref, gbr_ref, mhg_ref,
                  out_ref, qk_scr, st_scr, m_scr):
    ct = x_ref.shape[0]
    hd = MLSTM_HEAD_DIM

    @pl.when(pl.program_id(1) == 0)
    def _():
        qk_scr[...] = jnp.zeros_like(qk_scr)
        st_scr[...] = jnp.zeros_like(st_scr)
        m_scr[...] = jnp.zeros_like(m_scr)

    h32 = _rmsnorm(x_ref[...], ng_ref[...])
    hb = h32.astype(BF16)
    pb = _dot(hb, wb_ref[...])
    gates = _dot_f32(h32, wg_ref[...])

    qk_scr[0:SUBLANES, :] = qk_scr[ct:ct + SUBLANES, :]
    qk_scr[SUBLANES:, :] = pb[:, :2 * MLSTM_WIDTH]
    conv = cb_ref[...]
    for j in range(CONV_WIDTH):
        off = SUBLANES - (CONV_WIDTH - 1) + j
        conv = conv + cw_ref[j:j + 1, :] * qk_scr[off:off + ct, :]
    qk = _silu(conv)
    q = qk[:, :MLSTM_WIDTH].astype(BF16)
    k = (qk[:, MLSTM_WIDTH:] * (hd ** -0.5)).astype(BF16)
    v = pb[:, 2 * MLSTM_WIDTH:3 * MLSTM_WIDTH].astype(BF16)
    z = pb[:, 3 * MLSTM_WIDTH:]

    gc = gates + gbc_ref[...]
    gr = gates.T[:SUBLANES, :] + gbr_ref[...]

    def logsig(t):
        return jnp.minimum(t, 0.0) - jnp.log1p(jnp.exp(-jnp.abs(t)))

    ti = lax.broadcasted_iota(jnp.int32, (ct, ct), 0)
    si = lax.broadcasted_iota(jnp.int32, (ct, ct), 1)
    causal = si <= ti
    lower = causal.astype(F32)
    upper = (ti <= si).astype(F32)
    bcum_c = _dot_f32(lower, logsig(gc))
    bcum_r = _dot_f32(logsig(gr), upper)
    ones_v = jnp.ones((ct, hd), BF16)

    outs = []
    for hh in range(MLSTM_HEADS):
        lo = hh * hd
        a_c = bcum_c[:, MLSTM_HEADS + hh:MLSTM_HEADS + hh + 1]
        i_c = gc[:, hh:hh + 1]
        a_r = bcum_r[MLSTM_HEADS + hh:MLSTM_HEADS + hh + 1, :]
        i_r = gr[hh:hh + 1, :]
        m_prev = m_scr[hh:hh + 1, 0:1]
        b_last = a_c[ct - 1:ct, :]

        dmat = jnp.where(causal, a_c - a_r + i_r, -jnp.inf)
        inter = a_c + m_prev
        m = jnp.maximum(inter, jnp.max(dmat, axis=-1, keepdims=True))
        dexp = jnp.exp(dmat - m)
        inter_w = jnp.exp(inter - m)

        qh = q[:, lo:lo + hd]
        kh = k[:, lo:lo + hd]
        v_aug = jnp.concatenate([v[:, lo:lo + hd], ones_v], axis=1)
        s = (_dot_nt(qh, kh) * dexp).astype(BF16)
        st = st_scr[hh]
        numden = _dot(s, v_aug) + inter_w * _dot(qh, st.astype(BF16))
        num = numden[:, :hd]
        den = numden[:, hd:hd + 1]
        hout = num / jnp.maximum(jnp.abs(den), jnp.exp(-m))

        g_end = b_last - a_c + i_c
        m_next = jnp.maximum(b_last + m_prev, jnp.max(b_last - a_r + i_r, axis=-1, keepdims=True))
        w_s = jnp.exp(g_end - m_next)
        decay = jnp.exp(b_last + m_prev - m_next)
        kw = (kh.astype(F32) * w_s).T.astype(BF16)
        st_scr[hh] = decay * st + _dot(kw, v_aug)
        m_scr[hh:hh + 1, :] = jnp.broadcast_to(m_next, (1, LANES))

        hn = hout * lax.rsqrt(jnp.mean(hout * hout, axis=-1, keepdims=True) + EPS)
        outs.append(hn)
    hcat = jnp.concatenate(outs, axis=1) * mhg_ref[...]
    out_ref[...] = (hcat * _silu(z)).astype(out_ref.dtype)


def _mlstm_call(x2, ng, wb, wg, cw, cb, gbc, gbr, mhg, batch, seq):
    ct = MLSTM_TILE
    nt = seq // ct
    return pl.pallas_call(
        _mlstm_kernel,
        grid=(batch, nt),
        in_specs=[
            pl.BlockSpec((ct, D_MODEL), lambda b, i: (b * nt + i, 0)),
            _full(ng.shape), _full(wb.shape), _full(wg.shape), _full(cw.shape), _full(cb.shape),
            _full(gbc.shape), _full(gbr.shape), _full(mhg.shape),
        ],
        out_specs=pl.BlockSpec((ct, MLSTM_WIDTH), lambda b, i: (b * nt + i, 0)),
        out_shape=jax.ShapeDtypeStruct((batch * seq, MLSTM_WIDTH), BF16),
        scratch_shapes=[
            pltpu.VMEM((ct + SUBLANES, 2 * MLSTM_WIDTH), F32),
            pltpu.VMEM((MLSTM_HEADS, MLSTM_HEAD_DIM, 2 * MLSTM_HEAD_DIM), F32),
            pltpu.VMEM((SUBLANES, LANES), F32),
        ],
        compiler_params=_params("arbitrary", "arbitrary"),
        name="mlstm",
    )(x2, ng, wb, wg, cw, cb, gbc, gbr, mhg)


def _dsaprep_kernel(x_ref, ng_ref, wc_ref, qg_ref, wuq_ref, wqi_ref, kg_ref, kb_ref,
                    qh_ref, qit_ref, wt_ref, k_ref, v_ref, ki_ref, zc_ref):
    tm = x_ref.shape[0]
    hb = _rmsnorm(x_ref[...], ng_ref[...]).astype(BF16)
    pc = _dot(hb, wc_ref[...])
    cq = pc[:, :Q_LORA_RANK]
    o = Q_LORA_RANK
    k_ref[...] = pc[:, o:o + ATTN_HEAD_DIM].astype(BF16)
    v_ref[...] = pc[:, o + ATTN_HEAD_DIM:o + 2 * ATTN_HEAD_DIM].astype(BF16)
    o += 2 * ATTN_HEAD_DIM
    zc_ref[...] = _silu(pc[:, o:o + ATTN_WIDTH]).astype(BF16)
    o += ATTN_WIDTH
    tail = pc[:, o:o + LANES]
    kidx = tail[:, :IDX_HEAD_DIM]
    mu = jnp.mean(kidx, axis=-1, keepdims=True)
    kc = kidx - mu
    ki = kc * lax.rsqrt(jnp.mean(kc * kc, axis=-1, keepdims=True) + EPS) * kg_ref[...] + kb_ref[...]
    ki_ref[...] = ki.astype(BF16)
    wi = tail * ((IDX_HEADS ** -0.5) * (IDX_HEAD_DIM ** -0.5))

    cqn = _rmsnorm(cq, qg_ref[...]).astype(BF16)
    q = (_dot(cqn, wuq_ref[...]) * (ATTN_HEAD_DIM ** -0.5 * LOG2E)).astype(BF16)
    qi = _dot(cqn, wqi_ref[...])
    for b in range(tm // Q_BLOCK):
        rows = slice(b * Q_BLOCK, (b + 1) * Q_BLOCK)
        for hh in range(ATTN_HEADS):
            qh_ref[b, hh * Q_BLOCK:(hh + 1) * Q_BLOCK, :] = q[rows, hh * ATTN_HEAD_DIM:(hh + 1) * ATTN_HEAD_DIM]
        qit = qi[rows, :].T
        qit_ref[b] = jnp.concatenate(
            [qit[hh * IDX_HEAD_DIM:(hh + 1) * IDX_HEAD_DIM, :] for hh in range(IDX_HEADS)], axis=1).astype(BF16)
        wt_ref[b] = wi[rows, :].T[IDX_HEAD_DIM:IDX_HEAD_DIM + IDX_HEADS, :]


def _dsaprep_call(x2, ng, wc, qg, wuq, wqi, kg, kb):
    t = x2.shape[0]
    tm = PREP_TILE
    nq = tm // Q_BLOCK
    row = lambda i: (i, 0)
    blk = lambda i: (i, 0, 0)
    return pl.pallas_call(
        _dsaprep_kernel,
        grid=(t // tm,),
        in_specs=[
            pl.BlockSpec((tm, D_MODEL), row),
            _full(ng.shape), _full(wc.shape), _full(qg.shape), _full(wuq.shape), _full(wqi.shape),
            _full(kg.shape), _full(kb.shape),
        ],
        out_specs=[
            pl.BlockSpec((nq, ATTN_HEADS * Q_BLOCK, ATTN_HEAD_DIM), blk),
            pl.BlockSpec((nq, IDX_HEAD_DIM, IDX_HEADS * Q_BLOCK), blk),
            pl.BlockSpec((nq, IDX_HEADS, Q_BLOCK), blk),
            pl.BlockSpec((tm, ATTN_HEAD_DIM), row),
            pl.BlockSpec((tm, ATTN_HEAD_DIM), row),
            pl.BlockSpec((tm, IDX_HEAD_DIM), row),
            pl.BlockSpec((tm, ATTN_WIDTH), row),
        ],
        out_shape=[
            jax.ShapeDtypeStruct((t // Q_BLOCK, ATTN_HEADS * Q_BLOCK, ATTN_HEAD_DIM), BF16),
            jax.ShapeDtypeStruct((t // Q_BLOCK, IDX_HEAD_DIM, IDX_HEADS * Q_BLOCK), BF16),
            jax.ShapeDtypeStruct((t // Q_BLOCK, IDX_HEADS, Q_BLOCK), F32),
            jax.ShapeDtypeStruct((t, ATTN_HEAD_DIM), BF16),
            jax.ShapeDtypeStruct((t, ATTN_HEAD_DIM), BF16),
            jax.ShapeDtypeStruct((t, IDX_HEAD_DIM), BF16),
            jax.ShapeDtypeStruct((t, ATTN_WIDTH), BF16),
        ],
        compiler_params=_params("arbitrary"),
        name="dsaprep",
    )(x2, ng, wc, qg, wuq, wqi, kg, kb)


def _dsa_kernel(qh_ref, qit_ref, wt_ref, zc_ref, ki_ref, k_ref, v_ref, out_ref,
                sc_scr, sct_scr, lg_scr, m_scr, acc_scr, *, topk, seq):
    j = pl.program_id(1)
    qb = Q_BLOCK
    kt_rows = KEY_TILE
    groups = kt_rows // SUBLANES
    n_t = (j * qb + qb + kt_rows - 1) // kt_rows
    t_pos = j * qb + lax.broadcasted_iota(jnp.int32, (kt_rows, qb), 1)
    s_loc = lax.broadcasted_iota(jnp.int32, (kt_rows, qb), 0)
    kf = float(topk)

    def tile_rows(kt):
        if isinstance(kt, int):
            return pl.ds(kt * kt_rows, kt_rows)
        return pl.ds(pl.multiple_of(kt * kt_rows, kt_rows), kt_rows)

    def chunk_lanes(kt, c):
        if isinstance(kt, int):
            return pl.ds(kt * kt_rows + c * LANES, LANES)
        return pl.ds(pl.multiple_of(kt * kt_rows + c * LANES, LANES), LANES)

    def fold(fn, x):
        return _tree(fn, [x[g * SUBLANES:(g + 1) * SUBLANES, :] for g in range(groups)])

    qit = qit_ref[0]
    wt = wt_ref[0]

    def score_tile(kt, carry):
        mx, mn = carry
        rows = tile_rows(kt)
        z = _dot(ki_ref[rows, :], qit)
        sc = None
        for hh in range(IDX_HEADS):
            r = jnp.maximum(z[:, hh * qb:(hh + 1) * qb], 0.0) * wt[hh:hh + 1, :]
            sc = r if sc is None else sc + r
        causal = kt * kt_rows + s_loc <= t_pos
        masked = jnp.where(causal, sc, -jnp.inf)
        sc_scr[rows, :] = masked
        sct_scr[:, tile_rows(kt)] = masked.T
        mx = jnp.maximum(mx, fold(jnp.maximum, masked))
        mn = jnp.minimum(mn, fold(jnp.minimum, jnp.where(causal, sc, jnp.inf)))
        lg_scr[:, tile_rows(kt)] = _dot_nt(qh_ref[0], k_ref[rows, :])
        return mx, mn

    mx8, mn8 = lax.fori_loop(0, n_t, score_tile,
                             (jnp.full((SUBLANES, qb), -jnp.inf, F32), jnp.full((SUBLANES, qb), jnp.inf, F32)))
    row_max = jnp.max(mx8, axis=0, keepdims=True)
    row_min = jnp.min(mn8, axis=0, keepdims=True)

    def count_ge(thr):
        thr8 = jnp.broadcast_to(thr, (SUBLANES, qb))

        def step(kt, acc):
            sc = sc_scr[tile_rows(kt), :]
            hit = [jnp.where(sc[g * SUBLANES:(g + 1) * SUBLANES, :] >= thr8, 1.0, 0.0) for g in range(groups)]
            return acc + _tree(jnp.add, hit)

        acc = lax.fori_loop(0, n_t, step, jnp.zeros((SUBLANES, qb), F32))
        return jnp.sum(acc, axis=0, keepdims=True)

    def max_below(bound):
        def step(kt, acc):
            sc = sc_scr[tile_rows(kt), :]
            return jnp.maximum(acc, fold(jnp.maximum, jnp.where(sc < bound, sc, -jnp.inf)))

        acc = lax.fori_loop(0, n_t, step, jnp.full((SUBLANES, qb), -jnp.inf, F32))
        return jnp.max(acc, axis=0, keepdims=True)

    def flag(cond):
        return jnp.where(cond, 1.0, 0.0)

    n_causal = (t_pos[0:1, :] + 1).astype(F32)
    done0 = flag(n_causal <= kf)
    zero = jnp.zeros((1, qb), F32)

    def bis_cond(c):
        it, _, _, _, _, done, stall, _ = c
        return jnp.logical_and(it < MAX_BISECT, jnp.min(jnp.maximum(done, stall)) < 0.5)

    def bis_body(c):
        it, lo, hi, clo, chi, done, stall, same = c
        mid = 0.5 * lo + 0.5 * hi
        inside = jnp.logical_and(mid > lo, mid < hi)
        cnt = count_ge(mid)
        act = jnp.logical_and(jnp.maximum(done, stall) < 0.5, inside)
        up = jnp.logical_and(act, cnt >= kf)
        dn = jnp.logical_and(act, cnt < kf)
        nlo = jnp.where(up, mid, lo)
        nclo = jnp.where(up, cnt, clo)
        nhi = jnp.where(dn, mid, hi)
        nchi = jnp.where(dn, cnt, chi)
        same = jnp.where((nclo - nchi) == (clo - chi), same + 1.0, 0.0)
        done = jnp.maximum(done, flag(nclo == kf))
        stall = jnp.maximum(stall, flag(jnp.logical_or(jnp.logical_not(inside), same >= STALL_ITERS)))
        return it + 1, nlo, nhi, nclo, nchi, done, stall, same

    lo0 = row_min
    hi0 = row_max + (jnp.abs(row_max) + 1.0) * 1e-6
    state = (jnp.int32(0), lo0, hi0, n_causal, zero, done0, zero, zero)
    state = lax.fori_loop(0, BLIND_BISECT, lambda _, c: bis_body(c), state)
    _, lo, hi, clo, chi, done, _, _ = lax.while_loop(bis_cond, lambda c: bis_body(bis_body(c)), state)

    def walk_cond(c):
        it, _, _, _, _, done, _ = c
        return jnp.logical_and(it < seq, jnp.min(done) < 0.5)

    def walk_body(c):
        it, lo, hi, clo, chi, done, tie = c
        cand = max_below(hi)
        cge = count_ge(cand)
        act = done < 0.5
        fin = jnp.logical_and(act, cge >= kf)
        mv = jnp.logical_and(act, cge < kf)
        lo = jnp.where(fin, cand, lo)
        clo = jnp.where(fin, cge, clo)
        tie = jnp.maximum(tie, flag(jnp.logical_and(fin, cge > kf)))
        hi = jnp.where(mv, cand, hi)
        chi = jnp.where(mv, cge, chi)
        done = jnp.maximum(done, flag(fin))
        return it + 1, lo, hi, clo, chi, done, tie

    _, thr, _, _, chi, _, tie = lax.while_loop(
        walk_cond, walk_body, (jnp.int32(0), lo, hi, clo, chi, done, zero))
    thr8 = jnp.broadcast_to(thr, (SUBLANES, qb))

    @pl.when(jnp.max(tie) > 0.5)
    def _():
        need = jnp.where(tie > 0.5, kf - chi, float(seq))
        lower = (lax.broadcasted_iota(jnp.int32, (kt_rows, kt_rows), 1)
                 <= lax.broadcasted_iota(jnp.int32, (kt_rows, kt_rows), 0))
        lower = jnp.where(lower, 1.0, 0.0).astype(BF16)

        def step(kt, seen):
            rows = tile_rows(kt)
            sc = sc_scr[rows, :]
            eq = sc == thr
            rank = _dot(lower, jnp.where(eq, 1.0, 0.0).astype(BF16)) + seen
            kept = jnp.where(jnp.logical_and(eq, rank > need), -jnp.inf, sc)
            sc_scr[rows, :] = kept
            sct_scr[:, tile_rows(kt)] = kept.T
            return rank[kt_rows - 1:kt_rows, :]

        lax.fori_loop(0, n_t, step, zero)

    chunks = kt_rows // LANES
    thr_col = jnp.broadcast_to(thr, (qb, qb)).T

    m_scr[...] = jnp.full_like(m_scr, NEG_BIG)

    def max_tile(kt, carry):
        bias = []
        for c in range(chunks):
            lanes = chunk_lanes(kt, c)
            b = jnp.where(sct_scr[:, lanes] >= thr_col, 0.0, NEG_BIG)
            sct_scr[:, lanes] = b
            bias.append(b)
        for hh in range(ATTN_HEADS):
            hrows = slice(hh * qb, (hh + 1) * qb)
            m = m_scr[hrows, :]
            for c in range(chunks):
                m = jnp.maximum(m, lg_scr[hrows, chunk_lanes(kt, c)] + bias[c])
            m_scr[hrows, :] = m
        return carry

    lax.fori_loop(0, n_t, max_tile, 0)
    m_scr[...] = jnp.broadcast_to(jnp.max(m_scr[...], axis=1, keepdims=True), m_scr.shape)

    acc_scr[...] = jnp.zeros_like(acc_scr)
    ones_blk = jnp.ones((kt_rows, ATTN_HEAD_DIM), BF16)

    def attn_tile(kt, carry):
        v_aug = jnp.concatenate([v_ref[tile_rows(kt), :], ones_blk], axis=1)
        for hh in range(ATTN_HEADS):
            hrows = slice(hh * qb, (hh + 1) * qb)
            m = m_scr[hrows, :]
            p = [jnp.exp2(lg_scr[hrows, chunk_lanes(kt, c)] + sct_scr[:, chunk_lanes(kt, c)] - m).astype(BF16)
                 for c in range(chunks)]
            acc_scr[hrows, :] += _dot(jnp.concatenate(p, axis=1), v_aug)
        return carry

    lax.fori_loop(0, n_t, attn_tile, 0)
    acc = acc_scr[...]
    o = jnp.concatenate(
        [acc[hh * qb:(hh + 1) * qb, :ATTN_HEAD_DIM] / acc[hh * qb:(hh + 1) * qb, ATTN_HEAD_DIM:]
         for hh in range(ATTN_HEADS)], axis=1)
    out_ref[...] = (o * zc_ref[...].astype(F32)).astype(out_ref.dtype)


def _dsa_call(qh, qit, wt, zc, ki, k, v, batch, seq):
    nb = seq // Q_BLOCK
    topk = min(TOPK_MAX, seq // 4)
    blk3 = lambda b, j: (b * nb + j, 0, 0)
    blk2 = lambda b, j: (b * nb + j, 0)
    per_b = lambda b, j: (b, 0)
    return pl.pallas_call(
        functools.partial(_dsa_kernel, topk=topk, seq=seq),
        grid=(batch, nb),
        in_specs=[
            pl.BlockSpec((1, ATTN_HEADS * Q_BLOCK, ATTN_HEAD_DIM), blk3),
            pl.BlockSpec((1, IDX_HEAD_DIM, IDX_HEADS * Q_BLOCK), blk3),
            pl.BlockSpec((1, IDX_HEADS, Q_BLOCK), blk3),
            pl.BlockSpec((Q_BLOCK, ATTN_WIDTH), blk2),
            pl.BlockSpec((seq, IDX_HEAD_DIM), per_b),
            pl.BlockSpec((seq, ATTN_HEAD_DIM), per_b),
            pl.BlockSpec((seq, ATTN_HEAD_DIM), per_b),
        ],
        out_specs=pl.BlockSpec((Q_BLOCK, ATTN_WIDTH), blk2),
        out_shape=jax.ShapeDtypeStruct((batch * seq, ATTN_WIDTH), BF16),
        scratch_shapes=[
            pltpu.VMEM((seq, Q_BLOCK), F32),
            pltpu.VMEM((Q_BLOCK, seq + LANES), F32),
            pltpu.VMEM((ATTN_HEADS * Q_BLOCK, seq + LANES), F32),
            pltpu.VMEM((ATTN_HEADS * Q_BLOCK, LANES), F32),
            pltpu.VMEM((ATTN_HEADS * Q_BLOCK, 2 * ATTN_HEAD_DIM), F32),
        ],
        compiler_params=_params("arbitrary", "arbitrary"),
        name="dsa",
    )(qh, qit, wt, zc, ki, k, v)


def _merge_kernel(x_ref, ng_ref, wg_ref, ya_ref, yb_ref, yc_ref, wbr_ref, wo_ref, fg_ref, out_ref,
                  *, final_norm):
    x = x_ref[...]
    hb = _rmsnorm(x, ng_ref[...]).astype(BF16)
    merged = None
    for n, y_ref in enumerate((ya_ref, yb_ref, yc_ref)):
        gate = jax.nn.sigmoid(_dot(hb, wg_ref[:, n * D_MODEL:(n + 1) * D_MODEL]))
        term = gate * _dot(y_ref[...], wbr_ref[n])
        merged = term if merged is None else merged + term
    y = x + _dot(merged.astype(BF16), wo_ref[...])
    if final_norm:
        y = _rmsnorm(y, fg_ref[...])
    out_ref[...] = y


def _merge_call(x2, ng, wg, ya, yb, yc, wbr, wo, fg, final_norm):
    t = x2.shape[0]
    tm = MERGE_TILE
    row = lambda i: (i, 0)
    return pl.pallas_call(
        functools.partial(_merge_kernel, final_norm=final_norm),
        grid=(t // tm,),
        in_specs=[
            pl.BlockSpec((tm, D_MODEL), row),
            _full(ng.shape), _full(wg.shape),
            pl.BlockSpec((tm, SSM_WIDTH), row),
            pl.BlockSpec((tm, MLSTM_WIDTH), row),
            pl.BlockSpec((tm, ATTN_WIDTH), row),
            _full(wbr.shape), _full(wo.shape), _full(fg.shape),
        ],
        out_specs=pl.BlockSpec((tm, D_MODEL), row),
        out_shape=jax.ShapeDtypeStruct((t, D_MODEL), F32),
        compiler_params=_params("arbitrary"),
        name="merge",
    )(x2, ng, wg, ya, yb, yc, wbr, wo, fg)


def _row(v):
    return v.astype(F32).reshape(1, -1)


def _layer(x2, batch, seq, final_norm, norm_g, w_in, ssm_a_re, ssm_a_im, ssm_b_re, ssm_b_im,
           ssm_c_re, ssm_c_im, ssm_log_dt, ssm_d, glu_w, glu_b, qk_conv_w, qk_conv_b, igate_b,
           fgate_b, mh_norm_g, q_norm_g, w_uq, w_qidx, kidx_norm_g, kidx_norm_b, w_branch, w_out,
           final_norm_g):
    ng = _row(norm_g)
    o_a = 0
    o_b = o_a + 2 * SSM_WIDTH
    o_if = o_b + 3 * MLSTM_WIDTH
    o_zb = o_if + 2 * MLSTM_HEADS
    o_c = o_zb + MLSTM_WIDTH
    o_kidx = o_c + Q_LORA_RANK + 2 * ATTN_HEAD_DIM
    o_zc = o_kidx + IDX_HEAD_DIM + IDX_HEADS
    o_g = o_zc + ATTN_WIDTH

    wa = w_in[:, o_a:o_b].astype(BF16)
    wbu, cst, wc5 = _s5_tables(ssm_a_re, ssm_a_im, ssm_b_re, ssm_b_im, ssm_c_re, ssm_c_im, ssm_log_dt)
    ya = _s5_call(x2, ng, wa, wbu, cst, wc5, _row(ssm_d), glu_w.astype(BF16), _row(glu_b), batch, seq)

    wb = jnp.concatenate([w_in[:, o_b:o_if], w_in[:, o_zb:o_c]], axis=1).astype(BF16)
    wgt = jnp.pad(w_in[:, o_if:o_zb].astype(F32), ((0, 0), (0, LANES - 2 * MLSTM_HEADS)))
    gbias = jnp.concatenate([igate_b.astype(F32), fgate_b.astype(F32)])
    gbc = jnp.pad(gbias, (0, LANES - 2 * MLSTM_HEADS)).reshape(1, LANES)
    gbr = gbias.reshape(2 * MLSTM_HEADS, 1)
    yb = _mlstm_call(x2, ng, wb, wgt, qk_conv_w.astype(F32), _row(qk_conv_b), gbc, gbr,
                     _row(mh_norm_g), batch, seq)

    tail = jnp.pad(w_in[:, o_kidx:o_zc], ((0, 0), (0, LANES - IDX_HEAD_DIM - IDX_HEADS)))
    wcd = jnp.concatenate([w_in[:, o_c:o_kidx], w_in[:, o_zc:o_g], tail], axis=1).astype(BF16)
    qh, qit, wt, k, v, ki, zc = _dsaprep_call(
        x2, ng, wcd, _row(q_norm_g), w_uq.astype(BF16), w_qidx.astype(BF16),
        _row(kidx_norm_g), _row(kidx_norm_b))
    yc = _dsa_call(qh, qit, wt, zc, ki, k, v, batch, seq)

    wgm = w_in[:, o_g:].astype(BF16)
    return _merge_call(x2, ng, wgm, ya, yb, yc, w_branch.astype(BF16), w_out.astype(BF16),
                       _row(final_norm_g), final_norm)


def kernel(x, norm_g, w_in, ssm_a_re, ssm_a_im, ssm_b_re, ssm_b_im, ssm_c_re, ssm_c_im, ssm_log_dt, ssm_d, glu_w, glu_b, qk_conv_w, qk_conv_b, igate_b, fgate_b, mh_norm_g, q_norm_g, w_uq, w_qidx, kidx_norm_g, kidx_norm_b, w_branch, w_out, final_norm_g):
    batch, seq, d = x.shape
    depth = norm_g.shape[0]
    x2 = x.astype(F32).reshape(batch * seq, d)
    stacked = (norm_g, w_in, ssm_a_re, ssm_a_im, ssm_b_re, ssm_b_im, ssm_c_re, ssm_c_im, ssm_log_dt,
               ssm_d, glu_w, glu_b, qk_conv_w, qk_conv_b, igate_b, fgate_b, mh_norm_g, q_norm_g, w_uq,
               w_qidx, kidx_norm_g, kidx_norm_b, w_branch, w_out)
    for l in range(depth):
        x2 = _layer(x2, batch, seq, l == depth - 1, *[p[l] for p in stacked], final_norm_g)
    return x2.reshape(batch, seq, d).astype(x.dtype)
```
